```python
import jax, jax.numpy as jnp
from jax import lax
import numpy as np

D_MODEL = 4096
BATCH = 1
SEQ = 16384
DEPTH = 4

N_MIXERS = 3
BLOCK_Q = 128
RMS_EPS = 1e-6
N_MOD = 6

MLA_HEADS = 16
MLA_Q_LORA = 1536
MLA_KV_LORA = 512
MLA_D_NOPE = 128
MLA_D_ROPE = 64
MLA_D_V = 128
ROPE_THETA = 10000.0

SB_HEADS = 16
SB_HEAD_DIM = 128

POOL_WINDOWS = (2, 4, 8, 16)
POOL_GROUP = D_MODEL // len(POOL_WINDOWS)

D_FF = ((8 * D_MODEL + 3 * 256 - 1) // (3 * 256)) * 256

N_MLA = (DEPTH + N_MIXERS - 1) // N_MIXERS
N_SB = (DEPTH + N_MIXERS - 2) // N_MIXERS
N_POOL = (DEPTH + N_MIXERS - 3) // N_MIXERS

kernel_name = "hybrid_mla_stickbreak_pool_adaln"


def _rmsnorm(x, gain):
    xf = x.astype(jnp.float32)
    y = xf * lax.rsqrt(jnp.mean(xf * xf, axis=-1, keepdims=True) + RMS_EPS)
    return (y * gain.astype(jnp.float32)).astype(x.dtype)


def _modulate(x, gain, shift, scale):
    return _rmsnorm(x, gain) * (1 + scale[:, None, :]) + shift[:, None, :]


def _block_mask(blk, strict):
    length = (blk + 1) * BLOCK_Q
    q_idx = blk * BLOCK_Q + np.arange(BLOCK_Q)[:, None]
    k_idx = np.arange(length)[None, :]
    return (k_idx < q_idx) if strict else (k_idx <= q_idx)


def _prefix_sum(x):
    b, s, d = x.shape
    nb = s // BLOCK_Q
    x4 = x.reshape(b, nb, BLOCK_Q, d)
    upper = jnp.asarray(np.triu(np.ones((BLOCK_Q, BLOCK_Q), np.float32)), x.dtype)
    within = jnp.einsum('bnjd,jk->bnkd', x4, upper, precision=lax.Precision.HIGHEST)
    tot = jnp.sum(x4, axis=2)
    strict_up = jnp.asarray(np.triu(np.ones((nb, nb), np.float32), 1), x.dtype)
    off = jnp.einsum('bmd,mn->bnd', tot, strict_up, precision=lax.Precision.HIGHEST)
    return (within + off[:, :, None, :]).reshape(b, s, d)


def _rev_cumsum_last(u):
    length = u.shape[-1]
    nk = length // BLOCK_Q
    u4 = u.reshape(u.shape[:-1] + (nk, BLOCK_Q))
    lower = jnp.asarray(np.tril(np.ones((BLOCK_Q, BLOCK_Q), np.float32)), u.dtype)
    within = jnp.einsum('...nj,jk->...nk', u4, lower, precision=lax.Precision.HIGHEST)
    tot = jnp.sum(u4, axis=-1)
    strict_lo = jnp.asarray(np.tril(np.ones((nk, nk), np.float32), -1), u.dtype)
    after = jnp.einsum('...m,mn->...n', tot, strict_lo, precision=lax.Precision.HIGHEST)
    return (within + after[..., None]).reshape(u.shape)


def _rope_tables(positions):
    inv = 1.0 / (ROPE_THETA ** (jnp.arange(0, MLA_D_ROPE, 2, dtype=jnp.float32) / MLA_D_ROPE))
    ang = positions.astype(jnp.float32)[..., None] * inv
    ang = jnp.concatenate([ang, ang], axis=-1)
    return jnp.cos(ang), jnp.sin(ang)


def _rotate(x, cos, sin):
    half = x.shape[-1] // 2
    rot = jnp.concatenate([-x[..., half:], x[..., :half]], axis=-1)
    return (x * cos + rot * sin).astype(x.dtype)


def _mla_causal_attention(q, k, v):
    seq = k.shape[1]
    outs = []
    for blk in range(seq // BLOCK_Q):
        length = (blk + 1) * BLOCK_Q
        qb = q[:, blk * BLOCK_Q:length]
        s = jnp.einsum('bqhd,bkhd->bhqk', qb, k[:, :length]).astype(jnp.float32)
        s = jnp.where(_block_mask(blk, strict=False), s, -jnp.inf)
        e = jnp.exp(s - jnp.max(s, axis=-1, keepdims=True))
        denom = jnp.transpose(jnp.sum(e, axis=-1), (0, 2, 1))[..., None]
        o = jnp.einsum('bhqk,bkhd->bqhd', e.astype(v.dtype), v[:, :length])
        outs.append(o / denom.astype(v.dtype))
    return jnp.concatenate(outs, axis=1)


def _mla_mixer(h, positions, w_dq, q_norm, w_uq, w_dkv, kv_norm, w_ukv, w_o):
    b, s, _ = h.shape
    c_q = _rmsnorm(h @ w_dq, q_norm)
    q = (c_q @ w_uq).reshape(b, s, MLA_HEADS, MLA_D_NOPE + MLA_D_ROPE)
    q_nope, q_rope = q[..., :MLA_D_NOPE], q[..., MLA_D_NOPE:]
    kv_a = h @ w_dkv
    c_kv = _rmsnorm(kv_a[..., :MLA_KV_LORA], kv_norm)
    k_rope = kv_a[..., MLA_KV_LORA:]
    kv = (c_kv @ w_ukv).reshape(b, s, MLA_HEADS, MLA_D_NOPE + MLA_D_V)
    k_nope, v = kv[..., :MLA_D_NOPE], kv[..., MLA_D_NOPE:]
    cos, sin = _rope_tables(positions)
    q_rope = _rotate(q_rope, cos[:, :, None, :], sin[:, :, None, :])
    k_rope = _rotate(k_rope, cos, sin)
    scale = (MLA_D_NOPE + MLA_D_ROPE) ** -0.5
    q = jnp.concatenate([q_nope, q_rope], axis=-1) * scale
    k = jnp.concatenate(
        [k_nope, jnp.broadcast_to(k_rope[:, :, None, :], (b, s, MLA_HEADS, MLA_D_ROPE))], axis=-1)
    o = _mla_causal_attention(q, k, v)
    return o.reshape(b, s, MLA_HEADS * MLA_D_V) @ w_o


def _stick_breaking_attention(q, k, v):
    seq = k.shape[1]
    outs = []
    for blk in range(seq // BLOCK_Q):
        length = (blk + 1) * BLOCK_Q
        strict = _block_mask(blk, strict=True)
        z = jnp.einsum('bqhd,bkhd->bhqk', q[:, blk * BLOCK_Q:length],
                       k[:, :length]).astype(jnp.float32)
        sp = jnp.where(strict, jax.nn.softplus(z), 0.0)
        log_a = z - _rev_cumsum_last(sp)
        a = jnp.where(strict, jnp.exp(log_a), 0.0).astype(v.dtype)
        outs.append(jnp.einsum('bhqk,bkhd->bqhd', a, v[:, :length]))
    return jnp.concatenate(outs, axis=1)


def _sb_mixer(h, w_qkv, w_o):
    b, s, _ = h.shape
    qkv = (h @ w_qkv).reshape(b, s, 3, SB_HEADS, SB_HEAD_DIM)
    q = qkv[:, :, 0] * (SB_HEAD_DIM ** -0.5)
    o = _stick_breaking_attention(q, qkv[:, :, 1], qkv[:, :, 2])
    return o.reshape(b, s, SB_HEADS * SB_HEAD_DIM) @ w_o


def _pool_mixer(h, w_pool, pool_scale):
    b, s, d = h.shape
    hf = h.astype(jnp.float32)
    csum = _prefix_sum(hf)
    pos_count = jnp.arange(1, s + 1)
    parts = []
    for g, w in enumerate(POOL_WINDOWS):
        lo, hi = g * POOL_GROUP, (g + 1) * POOL_GROUP
        cg = csum[:, :, lo:hi]
        lag = jnp.pad(cg[:, :s - w], ((0, 0), (w, 0), (0, 0)))
        cnt = jnp.minimum(pos_count, w).astype(jnp.float32)[None, :, None]
        parts.append((cg - lag) / cnt - hf[:, :, lo:hi])
    p = jnp.stack(parts, axis=2).astype(h.dtype)
    y = jnp.einsum('bsgc,gcd->bsgd', p, w_pool).reshape(b, s, d)
    return y * pool_scale


def _swiglu(h, w_gate, w_up, w_down):
    return (jax.nn.silu(h @ w_gate) * (h @ w_up)) @ w_down


def setup_inputs(seed: int = 0) -> dict:
    key = jax.random.key(seed)
    ks = jax.random.split(key, 24)
    f32 = jnp.float32

    def nrm(k, shape, scale):
        return jax.random.normal(k, shape, f32) * scale

    def gain(k, shape):
        return 1.0 + 0.05 * jax.random.normal(k, shape, f32)

    D = D_MODEL
    x = jax.random.normal(ks[0], (BATCH, SEQ, D), f32)
    c = jax.random.normal(ks[1], (BATCH, D), f32)
    positions = (jnp.arange(SEQ, dtype=jnp.int32)[None, :]
                 + jax.random.randint(ks[2], (BATCH, 1), 0, 4096, dtype=jnp.int32))
    ada_w = nrm(ks[3], (D, N_MOD * D), 0.3 * D ** -0.5)
    ada_table = nrm(ks[4], (DEPTH, N_MOD, D), 0.2)
    norm_mix = gain(ks[5], (DEPTH, D))
    norm_ffn = gain(ks[6], (DEPTH, D))
    norm_final = gain(ks[7], (D,))
    mla_w_dq = nrm(ks[8], (N_MLA, D, MLA_Q_LORA), D ** -0.5)
    mla_q_norm = gain(ks[9], (N_MLA, MLA_Q_LORA))
    mla_w_uq = nrm(ks[10], (N_MLA, MLA_Q_LORA, MLA_HEADS * (MLA_D_NOPE + MLA_D_ROPE)), MLA_Q_LORA ** -0.5)
    mla_w_dkv = nrm(ks[11], (N_MLA, D, MLA_KV_LORA + MLA_D_ROPE), D ** -0.5)
    mla_kv_norm = gain(ks[12], (N_MLA, MLA_KV_LORA))
    mla_w_ukv = nrm(ks[13], (N_MLA, MLA_KV_LORA, MLA_HEADS * (MLA_D_NOPE + MLA_D_V)), MLA_KV_LORA ** -0.5)
    mla_w_o = nrm(ks[14], (N_MLA, MLA_HEADS * MLA_D_V, D), (MLA_HEADS * MLA_D_V) ** -0.5)
    sb_w_qkv = nrm(ks[15], (N_SB, D, 3 * SB_HEADS * SB_HEAD_DIM), D ** -0.5)
    sb_w_o = nrm(ks[16], (N_SB, SB_HEADS * SB_HEAD_DIM, D), (SB_HEADS * SB_HEAD_DIM) ** -0.5)
    pool_w = nrm(ks[17], (N_POOL, len(POOL_WINDOWS), POOL_GROUP, POOL_GROUP), POOL_GROUP ** -0.5)
    pool_scale = 0.5 + 0.1 * jax.random.normal(ks[18], (N_POOL, D), f32)
    ffn_w_gate = nrm(ks[19], (DEPTH, D, D_FF), D ** -0.5)
    ffn_w_up = nrm(ks[20], (DEPTH, D, D_FF), D ** -0.5)
    ffn_w_down = nrm(ks[21], (DEPTH, D_FF, D), D_FF ** -0.5)
    return {"x": x, "c": c, "positions": positions, "ada_w": ada_w, "ada_table": ada_table,
            "norm_mix": norm_mix, "norm_ffn": norm_ffn, "norm_final": norm_final,
            "mla_w_dq": mla_w_dq, "mla_q_norm": mla_q_norm, "mla_w_uq": mla_w_uq,
            "mla_w_dkv": mla_w_dkv, "mla_kv_norm": mla_kv_norm, "mla_w_ukv": mla_w_ukv,
            "mla_w_o": mla_w_o, "sb_w_qkv": sb_w_qkv, "sb_w_o": sb_w_o,
            "pool_w": pool_w, "pool_scale": pool_scale,
            "ffn_w_gate": ffn_w_gate, "ffn_w_up": ffn_w_up, "ffn_w_down": ffn_w_down}


def reference(x, c, positions, ada_w, ada_table, norm_mix, norm_ffn, norm_final,
              mla_w_dq, mla_q_norm, mla_w_uq, mla_w_dkv, mla_kv_norm, mla_w_ukv, mla_w_o,
              sb_w_qkv, sb_w_o, pool_w, pool_scale, ffn_w_gate, ffn_w_up, ffn_w_down):
    b = c.shape[0]
    base = (jax.nn.silu(c) @ ada_w).reshape(b, N_MOD, D_MODEL)
    for i in range(DEPTH):
        mod = base + ada_table[i][None]
        shift_m, scale_m, gate_m = mod[:, 0], mod[:, 1], mod[:, 2]
        shift_f, scale_f, gate_f = mod[:, 3], mod[:, 4], mod[:, 5]
        h = _modulate(x, norm_mix[i], shift_m, scale_m)
        kind, j = i % N_MIXERS, i // N_MIXERS
        if kind == 0:
            y = _mla_mixer(h, positions, mla_w_dq[j], mla_q_norm[j], mla_w_uq[j],
                           mla_w_dkv[j], mla_kv_norm[j], mla_w_ukv[j], mla_w_o[j])
        elif kind == 1:
            y = _sb_mixer(h, sb_w_qkv[j], sb_w_o[j])
        else:
            y = _pool_mixer(h, pool_w[j], pool_scale[j])
        x = x + gate_m[:, None, :] * y
        h = _modulate(x, norm_ffn[i], shift_f, scale_f)
        x = x + gate_f[:, None, :] * _swiglu(h, ffn_w_gate[i], ffn_w_up[i], ffn_w_down[i])
    return _rmsnorm(x, norm_final)
```

```python
import functools

import jax
import jax.numpy as jnp
from jax import lax
from jax.experimental import pallas as pl
from jax.experimental.pallas import tpu as pltpu

F32 = jnp.float32
BF16 = jnp.bfloat16

D_MODEL = 4096
SEQ = 16384
DEPTH = 4
N_MIXERS = 3
RMS_EPS = 1e-6
N_MOD = 6

MLA_HEADS = 16
MLA_Q_LORA = 1536
MLA_KV_LORA = 512
MLA_D_NOPE = 128
MLA_D_ROPE = 64
MLA_D_V = 128
ROPE_THETA = 10000.0
MLA_D_QK_PAD = 256
MLA_KVA_PAD = MLA_KV_LORA + 128

SB_HEADS = 16
SB_HEAD_DIM = 128

POOL_WINDOWS = (2, 4, 8, 16)
POOL_GROUP = D_MODEL // len(POOL_WINDOWS)
POOL_HALO = 16

D_FF = ((8 * D_MODEL + 3 * 256 - 1) // (3 * 256)) * 256
D_FF_PAD = 11264

LANES = 128
VMEM_LIMIT = 56 * 1024 * 1024


def _params(*sem):
    return pltpu.CompilerParams(dimension_semantics=sem, vmem_limit_bytes=VMEM_LIMIT)


def _ada_kernel(c_ref, w_ref, tab_ref, o_ref):
    c = c_ref[...]
    s = c * jax.nn.sigmoid(c)
    lhs = jnp.broadcast_to(s, (8, s.shape[1])).astype(BF16)
    base = jnp.dot(lhs, w_ref[...].astype(BF16), preferred_element_type=F32)
    o_ref[...] = base[0:1, :] + tab_ref[...]


def _ada_mod(c, ada_w, ada_table):
    d = c.shape[1]
    n = ada_w.shape[1]
    tn = 512
    tab = ada_table.reshape(DEPTH, n)
    return pl.pallas_call(
        _ada_kernel,
        out_shape=jax.ShapeDtypeStruct((DEPTH, n), F32),
        grid=(n // tn,),
        in_specs=[
            pl.BlockSpec((1, d), lambda j: (0, 0)),
            pl.BlockSpec((d, tn), lambda j: (0, j)),
            pl.BlockSpec((DEPTH, tn), lambda j: (0, j)),
        ],
        out_specs=pl.BlockSpec((DEPTH, tn), lambda j: (0, j)),
        compiler_params=_params("arbitrary"),
        name="ada_mod",
    )(c, ada_w, tab)


def _modulated(x, gain, shift, scale):
    ms = jnp.mean(x * x, axis=-1, keepdims=True)
    y = x * lax.rsqrt(ms + RMS_EPS) * gain
    return y * (1.0 + scale) + shift


def _modulate_kernel(x_ref, g_ref, sh_ref, sc_ref, o_ref):
    h = _modulated(x_ref[...], g_ref[...], sh_ref[...], sc_ref[...])
    o_ref[...] = h.astype(o_ref.dtype)


def _modulate(x, gain, shift, scale):
    s, d = x.shape
    tm = 256
    vec = pl.BlockSpec((1, d), lambda i: (0, 0))
    return pl.pallas_call(
        _modulate_kernel,
        out_shape=jax.ShapeDtypeStruct((s, d), BF16),
        grid=(s // tm,),
        in_specs=[pl.BlockSpec((tm, d), lambda i: (i, 0)), vec, vec, vec],
        out_specs=pl.BlockSpec((tm, d), lambda i: (i, 0)),
        compiler_params=_params("arbitrary"),
        name="modulate",
    )(x, gain, shift, scale)


def _final_norm_kernel(x_ref, g_ref, o_ref):
    x = x_ref[...]
    ms = jnp.mean(x * x, axis=-1, keepdims=True)
    o_ref[...] = x * lax.rsqrt(ms + RMS_EPS) * g_ref[...]


def _final_norm(x, gain):
    s, d = x.shape
    tm = 256
    return pl.pallas_call(
        _final_norm_kernel,
        out_shape=jax.ShapeDtypeStruct((s, d), F32),
        grid=(s // tm,),
        in_specs=[pl.BlockSpec((tm, d), lambda i: (i, 0)),
                  pl.BlockSpec((1, d), lambda i: (0, 0))],
        out_specs=pl.BlockSpec((tm, d), lambda i: (i, 0)),
        compiler_params=_params("arbitrary"),
        name="final_norm",
    )(x, gain)


def _mm_resid_kernel(a_ref, w_ref, x_ref, g_ref, o_ref, acc_ref, *, nk):
    k = pl.program_id(2)
    part = jnp.dot(a_ref[...], w_ref[...], preferred_element_type=F32)
    if nk == 1:
        o_ref[...] = x_ref[...] + g_ref[...] * part
        return

    @pl.when(k == 0)
    def _():
        acc_ref[...] = part

    @pl.when(jnp.logical_and(k > 0, k < nk - 1))
    def _():
        acc_ref[...] += part

    @pl.when(k == nk - 1)
    def _():
        o_ref[...] = x_ref[...] + g_ref[...] * (acc_ref[...] + part)


def _mm_resid(a, w, x, gate, *, tm, tn, tk, name):
    m, kdim = a.shape
    n = w.shape[1]
    nk = kdim // tk
    return pl.pallas_call(
        functools.partial(_mm_resid_kernel, nk=nk),
        out_shape=jax.ShapeDtypeStruct((m, n), F32),
        grid=(m // tm, n // tn, nk),
        in_specs=[
            pl.BlockSpec((tm, tk), lambda i, j, k: (i, k)),
            pl.BlockSpec((tk, tn), lambda i, j, k: (k, j)),
            pl.BlockSpec((tm, tn), lambda i, j, k: (i, j)),
            pl.BlockSpec((1, tn), lambda i, j, k: (0, j)),
        ],
        out_specs=pl.BlockSpec((tm, tn), lambda i, j, k: (i, j)),
        scratch_shapes=[pltpu.VMEM((tm, tn), F32)],
        compiler_params=_params("arbitrary", "arbitrary", "arbitrary"),
        name=name,
    )(a, w, x, gate)


def _mm_scale_kernel(a_ref, w_ref, s_ref, o_ref):
    acc = jnp.dot(a_ref[...], w_ref[...], preferred_element_type=F32)
    o_ref[...] = (acc * s_ref[...]).astype(o_ref.dtype)


def _mm_scale(a, w, colscale, *, tm, tn, name):
    m, kdim = a.shape
    n = w.shape[1]
    return pl.pallas_call(
        _mm_scale_kernel,
        out_shape=jax.ShapeDtypeStruct((m, n), BF16),
        grid=(m // tm, n // tn),
        in_specs=[
            pl.BlockSpec((tm, kdim), lambda i, j: (i, 0)),
            pl.BlockSpec((kdim, tn), lambda i, j: (0, j)),
            pl.BlockSpec((1, tn), lambda i, j: (0, j)),
        ],
        out_specs=pl.BlockSpec((tm, tn), lambda i, j: (i, j)),
        compiler_params=_params("arbitrary", "arbitrary"),
        name=name,
    )(a, w, colscale)


def _ffn_up_kernel(a_ref, wg_ref, wu_ref, o_ref):
    a = a_ref[...]
    g = jnp.dot(a, wg_ref[...], preferred_element_type=F32)
    u = jnp.dot(a, wu_ref[...], preferred_element_type=F32)
    o_ref[...] = (g * jax.nn.sigmoid(g) * u).astype(o_ref.dtype)


def _ffn_up(h, wg, wu, *, tm, tn):
    m, kdim = h.shape
    n = wg.shape[1]
    wspec = pl.BlockSpec((kdim, tn), lambda i, j: (0, j))
    return pl.pallas_call(
        _ffn_up_kernel,
        out_shape=jax.ShapeDtypeStruct((m, n), BF16),
        grid=(m // tm, n // tn),
        in_specs=[pl.BlockSpec((tm, kdim), lambda i, j: (i, 0)), wspec, wspec],
        out_specs=pl.BlockSpec((tm, tn), lambda i, j: (i, j)),
        compiler_params=_params("arbitrary", "arbitrary"),
        name="ffn_up",
    )(h, wg, wu)


def _rope_table_kernel(pos_ref, inv_ref, c_ref, s1_ref, s2_ref):
    ang = pos_ref[...] * inv_ref[...]
    cos = jnp.cos(ang)
    sin = jnp.sin(ang)
    lane = lax.broadcasted_iota(jnp.int32, ang.shape, 1)
    half = MLA_D_ROPE // 2
    c_ref[...] = jnp.where(lane < MLA_D_ROPE, cos, 0.0)
    s1_ref[...] = jnp.where(lane < half, -sin, 0.0)
    s2_ref[...] = jnp.where(jnp.logical_and(lane >= half, lane < MLA_D_ROPE), sin, 0.0)


def _rope_tables(positions):
    s = positions.shape[0]
    half = MLA_D_ROPE // 2
    inv = 1.0 / (ROPE_THETA ** (jnp.arange(0, MLA_D_ROPE, 2, dtype=F32) / MLA_D_ROPE))
    inv_lanes = jnp.concatenate([inv, inv, jnp.zeros((LANES - 2 * half,), F32)])[None, :]
    pos = positions.astype(F32)[:, None]
    tm = 1024
    out = jax.ShapeDtypeStruct((s, LANES), F32)
    tab = pl.BlockSpec((tm, LANES), lambda i: (i, 0))
    return pl.pallas_call(
        _rope_table_kernel,
        out_shape=(out, out, out),
        grid=(s // tm,),
        in_specs=[pl.BlockSpec((tm, 1), lambda i: (i, 0)),
                  pl.BlockSpec((1, LANES), lambda i: (0, 0))],
        out_specs=(tab, tab, tab),
        compiler_params=_params("arbitrary"),
        name="rope_tables",
    )(pos, inv_lanes)


def _rotate_lanes(r, c, s1, s2):
    return r * c + pltpu.roll(r, 96, axis=1) * s1 + pltpu.roll(r, 32, axis=1) * s2


def _mla_down_kernel(a_ref, wq_ref, wkv_ref, qn_ref, kvn_ref, c_ref, s1_ref, s2_ref,
                     cq_ref, ckv_ref, kr_ref, accq_ref, acckv_ref, *, nk):
    k = pl.program_id(1)
    a = a_ref[...]
    pq = jnp.dot(a, wq_ref[...], preferred_element_type=F32)
    pkv = jnp.dot(a, wkv_ref[...], preferred_element_type=F32)

    @pl.when(k == 0)
    def _():
        accq_ref[...] = pq
        acckv_ref[...] = pkv

    @pl.when(k > 0)
    def _():
        accq_ref[...] += pq
        acckv_ref[...] += pkv

    @pl.when(k == nk - 1)
    def _():
        cq = accq_ref[...]
        ms = jnp.mean(cq * cq, axis=-1, keepdims=True)
        cq_ref[...] = (cq * lax.rsqrt(ms + RMS_EPS) * qn_ref[...]).astype(cq_ref.dtype)
        ckv = acckv_ref[:, :MLA_KV_LORA]
        ms = jnp.mean(ckv * ckv, axis=-1, keepdims=True)
        ckv_ref[...] = (ckv * lax.rsqrt(ms + RMS_EPS) * kvn_ref[...]).astype(ckv_ref.dtype)
        kr = acckv_ref[:, MLA_KV_LORA:]
        kr_ref[...] = _rotate_lanes(kr, c_ref[...], s1_ref[...], s2_ref[...]).astype(kr_ref.dtype)


def _mla_down(h, w_dq, w_dkv_pad, q_norm, kv_norm, rope_c, rope_s1, rope_s2):
    m, kdim = h.shape
    tm, tk = 1024, 1024
    nk = kdim // tk
    nq = w_dq.shape[1]
    nkv = w_dkv_pad.shape[1]
    tab = pl.BlockSpec((tm, LANES), lambda i, k: (i, 0))
    return pl.pallas_call(
        functools.partial(_mla_down_kernel, nk=nk),
        out_shape=(jax.ShapeDtypeStruct((m, nq), BF16),
                   jax.ShapeDtypeStruct((m, MLA_KV_LORA), BF16),
                   jax.ShapeDtypeStruct((m, LANES), BF16)),
        grid=(m // tm, nk),
        in_specs=[
            pl.BlockSpec((tm, tk), lambda i, k: (i, k)),
            pl.BlockSpec((tk, nq), lambda i, k: (k, 0)),
            pl.BlockSpec((tk, nkv), lambda i, k: (k, 0)),
            pl.BlockSpec((1, nq), lambda i, k: (0, 0)),
            pl.BlockSpec((1, MLA_KV_LORA), lambda i, k: (0, 0)),
            tab, tab, tab,
        ],
        out_specs=(pl.BlockSpec((tm, nq), lambda i, k: (i, 0)),
                   pl.BlockSpec((tm, MLA_KV_LORA), lambda i, k: (i, 0)),
                   pl.BlockSpec((tm, LANES), lambda i, k: (i, 0))),
        scratch_shapes=[pltpu.VMEM((tm, nq), F32), pltpu.VMEM((tm, nkv), F32)],
        compiler_params=_params("arbitrary", "arbitrary"),
        name="mla_down",
    )(h, w_dq, w_dkv_pad, q_norm, kv_norm, rope_c, rope_s1, rope_s2)


def _mla_q_up_kernel(a_ref, w_ref, c_ref, s1_ref, s2_ref, o_ref, *, heads_per_tile):
    acc = jnp.dot(a_ref[...], w_ref[...], preferred_element_type=F32)
    scale = (MLA_D_NOPE + MLA_D_ROPE) ** -0.5
    c, s1, s2 = c_ref[...], s1_ref[...], s2_ref[...]
    for hh in range(heads_per_tile):
        lo = hh * MLA_D_QK_PAD
        o_ref[:, lo:lo + LANES] = (acc[:, lo:lo + LANES] * scale).astype(o_ref.dtype)
        r = _rotate_lanes(acc[:, lo + LANES:lo + 2 * LANES], c, s1, s2)
        o_ref[:, lo + LANES:lo + 2 * LANES] = (r * scale).astype(o_ref.dtype)


def _mla_q_up(cq, w_uq_pad, rope_c, rope_s1, rope_s2):
    m, kdim = cq.shape
    n = w_uq_pad.shape[1]
    tm, tn = 512, 1024
    tab = pl.BlockSpec((tm, LANES), lambda i, j: (i, 0))
    return pl.pallas_call(
        functools.partial(_mla_q_up_kernel, heads_per_tile=tn // MLA_D_QK_PAD),
        out_shape=jax.ShapeDtypeStruct((m, n), BF16),
        grid=(m // tm, n // tn),
        in_specs=[pl.BlockSpec((tm, kdim), lambda i, j: (i, 0)),
                  pl.BlockSpec((kdim, tn), lambda i, j: (0, j)),
                  tab, tab, tab],
        out_specs=pl.BlockSpec((tm, tn), lambda i, j: (i, j)),
        compiler_params=_params("arbitrary", "arbitrary"),
        name="mla_q_up",
    )(cq, w_uq_pad, rope_c, rope_s1, rope_s2)


def _mla_kv_up_kernel(a_ref, wk_ref, wv_ref, kr_ref, k_ref, v_ref):
    a = a_ref[...]
    kn = jnp.dot(a, wk_ref[...], preferred_element_type=F32)
    v_ref[...] = jnp.dot(a, wv_ref[...], preferred_element_type=F32).astype(v_ref.dtype)
    kr = kr_ref[...]
    for hh in range(MLA_HEADS):
        lo = hh * MLA_D_QK_PAD
        k_ref[:, lo:lo + LANES] = kn[:, hh * LANES:(hh + 1) * LANES].astype(k_ref.dtype)
        k_ref[:, lo + LANES:lo + 2 * LANES] = kr


def _mla_kv_up(ckv, w_uk, w_uv, k_rope):
    m, kdim = ckv.shape
    tm = 512
    nk = w_uk.shape[1]
    nv = w_uv.shape[1]
    return pl.pallas_call(
        _mla_kv_up_kernel,
        out_shape=(jax.ShapeDtypeStruct((m, MLA_HEADS * MLA_D_QK_PAD), BF16),
                   jax.ShapeDtypeStruct((m, nv), BF16)),
        grid=(m // tm,),
        in_specs=[pl.BlockSpec((tm, kdim), lambda i: (i, 0)),
                  pl.BlockSpec((kdim, nk), lambda i: (0, 0)),
                  pl.BlockSpec((kdim, nv), lambda i: (0, 0)),
                  pl.BlockSpec((tm, LANES), lambda i: (i, 0))],
        out_specs=(pl.BlockSpec((tm, MLA_HEADS * MLA_D_QK_PAD), lambda i: (i, 0)),
                   pl.BlockSpec((tm, nv), lambda i: (i, 0))),
        compiler_params=_params("arbitrary"),
        name="mla_kv_up",
    )(ckv, w_uk, w_uv, k_rope)


def _qk(q, k):
    return lax.dot_general(q, k, (((1,), (1,)), ((), ())), preferred_element_type=F32)


def _mla_attn_kernel(q_ref, k_ref, v_ref, o_ref, *, tq, tk):
    i = pl.program_id(1)
    q = q_ref[...]
    ndiag = tq // tk
    row = i * tq + lax.broadcasted_iota(jnp.int32, (tq, tk), 0)

    def step(j, m, l, acc, masked):
        start = pl.multiple_of(j * tk, tk)
        s = _qk(q, k_ref[pl.ds(start, tk), :])
        if masked:
            col = j * tk + lax.broadcasted_iota(jnp.int32, (tq, tk), 1)
            s = jnp.where(col <= row, s, -jnp.inf)
        m_new = jnp.maximum(m, jnp.max(s, axis=-1, keepdims=True))
        alpha = jnp.exp(m - m_new)
        p = jnp.exp(s - m_new)
        l = alpha * l + jnp.sum(p, axis=-1, keepdims=True)
        pv = jnp.dot(p.astype(BF16), v_ref[pl.ds(start, tk), :], preferred_element_type=F32)
        return m_new, l, alpha * acc + pv

    m = jnp.full((tq, 1), -jnp.inf, F32)
    l = jnp.zeros((tq, 1), F32)
    acc = jnp.zeros((tq, MLA_D_V), F32)
    first = i * ndiag
    m, l, acc = step(first, m, l, acc, True)
    for d in range(1, ndiag):
        m, l, acc = step(first + d, m, l, acc, True)

    def body(j, carry):
        return step(j, *carry, False)

    m, l, acc = lax.fori_loop(0, first, body, (m, l, acc))
    o_ref[...] = (acc / l).astype(o_ref.dtype)


def _mla_attention(q, k, v, *, tq, tk):
    s = q.shape[0]
    return pl.pallas_call(
        functools.partial(_mla_attn_kernel, tq=tq, tk=tk),
        out_shape=jax.ShapeDtypeStruct((s, MLA_HEADS * MLA_D_V), BF16),
        grid=(MLA_HEADS, s // tq),
        in_specs=[pl.BlockSpec((tq, MLA_D_QK_PAD), lambda h, i: (i, h)),
                  pl.BlockSpec((s, MLA_D_QK_PAD), lambda h, i: (0, h)),
                  pl.BlockSpec((s, MLA_D_V), lambda h, i: (0, h))],
        out_specs=pl.BlockSpec((tq, MLA_D_V), lambda h, i: (i, h)),
        compiler_params=_params("arbitrary", "arbitrary"),
        name="mla_attention",
    )(q, k, v)


def _sb_attn_kernel(q_ref, k_ref, v_ref, o_ref, *, tq, tk):
    i = pl.program_id(1)
    q = q_ref[...]
    ndiag = tq // tk
    row = i * tq + lax.broadcasted_iota(jnp.int32, (tq, tk), 0)
    tri = (lax.broadcasted_iota(jnp.int32, (tk, tk), 0)
           >= lax.broadcasted_iota(jnp.int32, (tk, tk), 1)).astype(BF16)

    def step(j, carry, acc, masked):
        start = pl.multiple_of(j * tk, tk)
        z = _qk(q, k_ref[pl.ds(start, tk), :])
        sp = jnp.maximum(z, 0.0) + jnp.log(1.0 + jnp.exp(-jnp.abs(z)))
        if masked:
            col = j * tk + lax.broadcasted_iota(jnp.int32, (tq, tk), 1)
            strict = col < row
            sp = jnp.where(strict, sp, 0.0)
        hi = sp.astype(BF16)
        lo = (sp - hi.astype(F32)).astype(BF16)
        cum = (jnp.dot(hi, tri, preferred_element_type=F32)
               + jnp.dot(lo, tri, preferred_element_type=F32)) + carry
        a = jnp.exp(z - cum)
        if masked:
            a = jnp.where(strict, a, 0.0)
        acc = acc + jnp.dot(a.astype(BF16), v_ref[pl.ds(start, tk), :],
                            preferred_element_type=F32)
        return carry + jnp.sum(sp, axis=-1, keepdims=True), acc

    carry = jnp.zeros((tq, 1), F32)
    acc = jnp.zeros((tq, SB_HEAD_DIM), F32)
    first = i * ndiag
    for d in reversed(range(ndiag)):
        carry, acc = step(first + d, carry, acc, True)

    def body(jj, c):
        return step(first - 1 - jj, *c, False)

    carry, acc = lax.fori_loop(0, first, body, (carry, acc))
    o_ref[...] = acc.astype(o_ref.dtype)


def _sb_attention(qkv, *, tq, tk):
    s = qkv.shape[0]
    hd = SB_HEAD_DIM
    return pl.pallas_call(
        functools.partial(_sb_attn_kernel, tq=tq, tk=tk),
        out_shape=jax.ShapeDtypeStruct((s, SB_HEADS * hd), BF16),
        grid=(SB_HEADS, s // tq),
        in_specs=[pl.BlockSpec((tq, hd), lambda h, i: (i, h)),
                  pl.BlockSpec((s, hd), lambda h, i: (0, SB_HEADS + h)),
                  pl.BlockSpec((s, hd), lambda h, i: (0, 2 * SB_HEADS + h))],
        out_specs=pl.BlockSpec((tq, hd), lambda h, i: (i, h)),
        compiler_params=_params("arbitrary", "arbitrary"),
        name="sb_attention",
    )(qkv, qkv, qkv)


def _pool_kernel(x_ref, halo_ref, g_ref, sh_ref, sc_ref, w_ref, ps_ref, gate_ref,
                 o_ref, hbuf_ref, *, tm):
    i = pl.program_id(0)
    g, sh, sc = g_ref[...], sh_ref[...], sc_ref[...]
    x = x_ref[...]
    h = _modulated(x, g, sh, sc)
    h_halo = _modulated(halo_ref[...], g, sh, sc)
    hbuf_ref[:POOL_HALO, :] = jnp.where(i > 0, h_halo, 0.0)
    hbuf_ref[POOL_HALO:, :] = h
    t = i * tm + lax.broadcasted_iota(jnp.int32, (tm, 1), 0)
    for gi, w in enumerate(POOL_WINDOWS):
        lo, hi = gi * POOL_GROUP, (gi + 1) * POOL_GROUP
        ssum = h[:, lo:hi]
        for k in range(1, w):
            ssum = ssum + hbuf_ref[POOL_HALO - k:POOL_HALO - k + tm, lo:hi]
        cnt = jnp.minimum(t + 1, w).astype(F32)
        p = ssum / cnt - h[:, lo:hi]
        y = jnp.dot(p.astype(BF16), w_ref[gi], preferred_element_type=F32)
        o_ref[:, lo:hi] = x[:, lo:hi] + gate_ref[:, lo:hi] * (y * ps_ref[:, lo:hi])


def _pool_layer(x, gain, shift, scale, w_pool, pool_scale, gate):
    s, d = x.shape
    tm = 256
    ng = len(POOL_WINDOWS)
    vec = pl.BlockSpec((1, d), lambda i: (0, 0))
    halo_blocks = tm // POOL_HALO
    return pl.pallas_call(
        functools.partial(_pool_kernel, tm=tm),
        out_shape=jax.ShapeDtypeStruct((s, d), F32),
        grid=(s // tm,),
        in_specs=[pl.BlockSpec((tm, d), lambda i: (i, 0)),
                  pl.BlockSpec((POOL_HALO, d),
                               lambda i: (jnp.maximum(i * halo_blocks - 1, 0), 0)),
                  vec, vec, vec,
                  pl.BlockSpec((ng, POOL_GROUP, POOL_GROUP), lambda i: (0, 0, 0)),
                  vec, vec],
        out_specs=pl.BlockSpec((tm, d), lambda i: (i, 0)),
        scratch_shapes=[pltpu.VMEM((tm + POOL_HALO, d), F32)],
        compiler_params=_params("arbitrary"),
        name="pool_layer",
    )(x, x, gain, shift, scale, w_pool, pool_scale, gate)


def _pad_cols(w, n):
    return jnp.pad(w, ((0, 0), (0, n - w.shape[1])))


def _mla_weights(w_dq, w_uq, w_dkv, w_ukv, w_o):
    h = MLA_HEADS
    w_uq_h = w_uq.reshape(MLA_Q_LORA, h, MLA_D_NOPE + MLA_D_ROPE)
    w_uq_pad = jnp.pad(w_uq_h, ((0, 0), (0, 0), (0, MLA_D_QK_PAD - MLA_D_NOPE - MLA_D_ROPE)))
    w_uq_pad = w_uq_pad.reshape(MLA_Q_LORA, h * MLA_D_QK_PAD).astype(BF16)
    w_ukv_h = w_ukv.reshape(MLA_KV_LORA, h, MLA_D_NOPE + MLA_D_V)
    w_uk = w_ukv_h[:, :, :MLA_D_NOPE].reshape(MLA_KV_LORA, h * MLA_D_NOPE).astype(BF16)
    w_uv = w_ukv_h[:, :, MLA_D_NOPE:].reshape(MLA_KV_LORA, h * MLA_D_V).astype(BF16)
    w_dkv_pad = _pad_cols(w_dkv, MLA_KVA_PAD).astype(BF16)
    return w_dq.astype(BF16), w_uq_pad, w_dkv_pad, w_uk, w_uv, w_o.astype(BF16)


def _mla_layer(x, h, rope, w_dq, q_norm, w_uq, w_dkv, kv_norm, w_ukv, w_o, gate):
    w_dq_b, w_uq_pad, w_dkv_pad, w_uk, w_uv, w_o_b = _mla_weights(w_dq, w_uq, w_dkv, w_ukv, w_o)
    cq, ckv, k_rope = _mla_down(h, w_dq_b, w_dkv_pad, q_norm[None, :], kv_norm[None, :], *rope)
    q = _mla_q_up(cq, w_uq_pad, *rope)
    k, v = _mla_kv_up(ckv, w_uk, w_uv, k_rope)
    o = _mla_attention(q, k, v, tq=512, tk=512)
    return _mm_resid(o, w_o_b, x, gate, tm=1024, tn=1024, tk=o.shape[1], name="mla_out")


def _sb_layer(x, h, w_qkv, w_o, gate):
    n = w_qkv.shape[1]
    colscale = jnp.where(jnp.arange(n) < SB_HEADS * SB_HEAD_DIM,
                         SB_HEAD_DIM ** -0.5, 1.0).astype(F32)[None, :]
    qkv = _mm_scale(h, w_qkv.astype(BF16), colscale, tm=1024, tn=1024, name="sb_qkv")
    o = _sb_attention(qkv, tq=256, tk=256)
    return _mm_resid(o, w_o.astype(BF16), x, gate, tm=1024, tn=1024, tk=o.shape[1],
                     name="sb_out")


def _ffn_layer(x, h, w_gate, w_up, w_down, gate):
    wg = _pad_cols(w_gate.astype(BF16), D_FF_PAD)
    wu = _pad_cols(w_up.astype(BF16), D_FF_PAD)
    wd = jnp.pad(w_down.astype(BF16), ((0, D_FF_PAD - D_FF), (0, 0)))
    act = _ffn_up(h, wg, wu, tm=1024, tn=512)
    return _mm_resid(act, wd, x, gate, tm=1024, tn=1024, tk=2816, name="ffn_down")


def kernel(x, c, positions, ada_w, ada_table, norm_mix, norm_ffn, norm_final,
           mla_w_dq, mla_q_norm, mla_w_uq, mla_w_dkv, mla_kv_norm, mla_w_ukv, mla_w_o,
           sb_w_qkv, sb_w_o, pool_w, pool_scale, ffn_w_gate, ffn_w_up, ffn_w_down):
    b, s, d = x.shape
    assert b == 1 and c.shape == (1, d)
    xs = x.reshape(s, d)
    mod = _ada_mod(c, ada_w, ada_table).reshape(DEPTH, N_MOD, 1, d)
    rope = _rope_tables(positions.reshape(s))
    for i in range(DEPTH):
        shift_m, scale_m, gate_m = mod[i, 0], mod[i, 1], mod[i, 2]
        shift_f, scale_f, gate_f = mod[i, 3], mod[i, 4], mod[i, 5]
        gain_m = norm_mix[i][None, :]
        kind, j = i % N_MIXERS, i // N_MIXERS
        if kind == 0:
            h = _modulate(xs, gain_m, shift_m, scale_m)
            xs = _mla_layer(xs, h, rope, mla_w_dq[j], mla_q_norm[j], mla_w_uq[j],
                            mla_w_dkv[j], mla_kv_norm[j], mla_w_ukv[j], mla_w_o[j], gate_m)
        elif kind == 1:
            h = _modulate(xs, gain_m, shift_m, scale_m)
            xs = _sb_layer(xs, h, sb_w_qkv[j], sb_w_o[j], gate_m)
        else:
            xs = _pool_layer(xs, gain_m, shift_m, scale_m, pool_w[j].astype(BF16),
                             pool_scale[j][None, :], gate_m)
        h = _modulate(xs, norm_ffn[i][None, :], shift_f, scale_f)
        xs = _ffn_layer(xs, h, ffn_w_gate[i], ffn_w_up[i], ffn_w_down[i], gate_f)
    return _final_norm(xs, norm_final[None, :]).reshape(b, s, d)
```

```python
import functools

import jax
import jax.numpy as jnp
from jax import lax
from jax.experimental import pallas as pl
from jax.experimental.pallas import tpu as pltpu

F32 = jnp.float32
BF16 = jnp.bfloat16

D_MODEL = 4096
SEQ = 16384
DEPTH = 4
N_MIXERS = 3
RMS_EPS = 1e-6
N_MOD = 6

MLA_HEADS = 16
MLA_Q_LORA = 1536
MLA_KV_LORA = 512
MLA_D_NOPE = 128
MLA_D_ROPE = 64
MLA_D_V = 128
ROPE_THETA = 10000.0
MLA_D_QK_PAD = 256
MLA_KVA_PAD = MLA_KV_LORA + 128

SB_HEADS = 16
SB_HEAD_DIM = 128

POOL_WINDOWS = (2, 4, 8, 16)
POOL_GROUP = D_MODEL // len(POOL_WINDOWS)
POOL_HALO = 16

D_FF = ((8 * D_MODEL + 3 * 256 - 1) // (3 * 256)) * 256
D_FF_PAD = 11264

LANES = 128
LOG2_E = 1.4426950408889634
VMEM_LIMIT = 56 * 1024 * 1024


def _params(*sem):
    return pltpu.CompilerParams(dimension_semantics=sem, vmem_limit_bytes=VMEM_LIMIT)


def _ada_kernel(c_ref, w_ref, tab_ref, o_ref):
    c = c_ref[...]
    s = c * jax.nn.sigmoid(c)
    lhs = jnp.broadcast_to(s, (8, s.shape[1])).astype(BF16)
    base = jnp.dot(lhs, w_ref[...].astype(BF16), preferred_element_type=F32)
    o_ref[...] = base[0:1, :] + tab_ref[...]


def _ada_mod(c, ada_w, ada_table):
    d = c.shape[1]
    n = ada_w.shape[1]
    tn = 512
    tab = ada_table.reshape(DEPTH, n)
    return pl.pallas_call(
        _ada_kernel,
        out_shape=jax.ShapeDtypeStruct((DEPTH, n), F32),
        grid=(n // tn,),
        in_specs=[
            pl.BlockSpec((1, d), lambda j: (0, 0)),
            pl.BlockSpec((d, tn), lambda j: (0, j)),
            pl.BlockSpec((DEPTH, tn), lambda j: (0, j)),
        ],
        out_specs=pl.BlockSpec((DEPTH, tn), lambda j: (0, j)),
        compiler_params=_params("arbitrary"),
        name="ada_mod",
    )(c, ada_w, tab)


def _modulated(x, gain, shift, scale):
    ms = jnp.mean(x * x, axis=-1, keepdims=True)
    y = x * lax.rsqrt(ms + RMS_EPS) * gain
    return y * (1.0 + scale) + shift


def _modulate_kernel(x_ref, g_ref, sh_ref, sc_ref, o_ref):
    h = _modulated(x_ref[...], g_ref[...], sh_ref[...], sc_ref[...])
    o_ref[...] = h.astype(o_ref.dtype)


def _modulate(x, gain, shift, scale):
    s, d = x.shape
    tm = 256
    vec = pl.BlockSpec((1, d), lambda i: (0, 0))
    return pl.pallas_call(
        _modulate_kernel,
        out_shape=jax.ShapeDtypeStruct((s, d), BF16),
        grid=(s // tm,),
        in_specs=[pl.BlockSpec((tm, d), lambda i: (i, 0)), vec, vec, vec],
        out_specs=pl.BlockSpec((tm, d), lambda i: (i, 0)),
        compiler_params=_params("arbitrary"),
        name="modulate",
    )(x, gain, shift, scale)


def _final_norm_kernel(x_ref, g_ref, o_ref):
    x = x_ref[...]
    ms = jnp.mean(x * x, axis=-1, keepdims=True)
    o_ref[...] = x * lax.rsqrt(ms + RMS_EPS) * g_ref[...]


def _final_norm(x, gain):
    s, d = x.shape
    tm = 256
    return pl.pallas_call(
        _final_norm_kernel,
        out_shape=jax.ShapeDtypeStruct((s, d), F32),
        grid=(s // tm,),
        in_specs=[pl.BlockSpec((tm, d), lambda i: (i, 0)),
                  pl.BlockSpec((1, d), lambda i: (0, 0))],
        out_specs=pl.BlockSpec((tm, d), lambda i: (i, 0)),
        compiler_params=_params("arbitrary"),
        name="final_norm",
    )(x, gain)


def _mm_resid_kernel(a_ref, w_ref, x_ref, g_ref, o_ref, acc_ref, *, nk):
    k = pl.program_id(2)
    part = jnp.dot(a_ref[...], w_ref[...], preferred_element_type=F32)
    if nk == 1:
        o_ref[...] = x_ref[...] + g_ref[...] * part
        return

    @pl.when(k == 0)
    def _():
        acc_ref[...] = part

    @pl.when(jnp.logical_and(k > 0, k < nk - 1))
    def _():
        acc_ref[...] += part

    @pl.when(k == nk - 1)
    def _():
        o_ref[...] = x_ref[...] + g_ref[...] * (acc_ref[...] + part)


def _mm_resid(a, w, x, gate, *, tm, tn, tk, name, layer=None):
    m, kdim = a.shape
    n = w.shape[-1]
    nk = kdim // tk
    if layer is None:
        wspec = pl.BlockSpec((tk, tn), lambda i, j, k: (k, j))
    else:
        wspec = pl.BlockSpec((None, tk, tn), lambda i, j, k: (layer, k, j))
    return pl.pallas_call(
        functools.partial(_mm_resid_kernel, nk=nk),
        out_shape=jax.ShapeDtypeStruct((m, n), F32),
        grid=(m // tm, n // tn, nk),
        in_specs=[
            pl.BlockSpec((tm, tk), lambda i, j, k: (i, k)),
            wspec,
            pl.BlockSpec((tm, tn), lambda i, j, k: (i, j)),
            pl.BlockSpec((1, tn), lambda i, j, k: (0, j)),
        ],
        out_specs=pl.BlockSpec((tm, tn), lambda i, j, k: (i, j)),
        scratch_shapes=[pltpu.VMEM((tm, tn), F32)],
        compiler_params=_params("arbitrary", "arbitrary", "arbitrary"),
        name=name,
    )(a, w, x, gate)


def _mm_scale_kernel(a_ref, w_ref, s_ref, o_ref):
    acc = jnp.dot(a_ref[...], w_ref[...], preferred_element_type=F32)
    o_ref[...] = (acc * s_ref[...]).astype(o_ref.dtype)


def _mm_scale(a, w, colscale, *, tm, tn, name):
    m, kdim = a.shape
    n = w.shape[1]
    return pl.pallas_call(
        _mm_scale_kernel,
        out_shape=jax.ShapeDtypeStruct((m, n), BF16),
        grid=(m // tm, n // tn),
        in_specs=[
            pl.BlockSpec((tm, kdim), lambda i, j: (i, 0)),
            pl.BlockSpec((kdim, tn), lambda i, j: (0, j)),
            pl.BlockSpec((1, tn), lambda i, j: (0, j)),
        ],
        out_specs=pl.BlockSpec((tm, tn), lambda i, j: (i, j)),
        compiler_params=_params("arbitrary", "arbitrary"),
        name=name,
    )(a, w, colscale)


def _ffn_up_kernel(a_ref, wg_ref, wu_ref, o_ref):
    a = a_ref[...]
    g = jnp.dot(a, wg_ref[...], preferred_element_type=F32)
    u = jnp.dot(a, wu_ref[...], preferred_element_type=F32)
    o_ref[...] = (g * jax.nn.sigmoid(g) * u).astype(o_ref.dtype)


def _ffn_up(h, wg, wu, layer, *, tm, tn):
    m, kdim = h.shape
    n = wg.shape[-1]
    wspec = pl.BlockSpec((None, kdim, tn), lambda i, j: (layer, 0, j))
    return pl.pallas_call(
        _ffn_up_kernel,
        out_shape=jax.ShapeDtypeStruct((m, n), BF16),
        grid=(m // tm, n // tn),
        in_specs=[pl.BlockSpec((tm, kdim), lambda i, j: (i, 0)), wspec, wspec],
        out_specs=pl.BlockSpec((tm, tn), lambda i, j: (i, j)),
        compiler_params=_params("arbitrary", "arbitrary"),
        name="ffn_up",
    )(h, wg, wu)


def _rope_table_kernel(pos_ref, inv_ref, c_ref, s1_ref, s2_ref):
    ang = pos_ref[...] * inv_ref[...]
    cos = jnp.cos(ang)
    sin = jnp.sin(ang)
    lane = lax.broadcasted_iota(jnp.int32, ang.shape, 1)
    half = MLA_D_ROPE // 2
    c_ref[...] = jnp.where(lane < MLA_D_ROPE, cos, 0.0)
    s1_ref[...] = jnp.where(lane < half, -sin, 0.0)
    s2_ref[...] = jnp.where(jnp.logical_and(lane >= half, lane < MLA_D_ROPE), sin, 0.0)


def _rope_tables(positions):
    s = positions.shape[0]
    half = MLA_D_ROPE // 2
    inv = 1.0 / (ROPE_THETA ** (jnp.arange(0, MLA_D_ROPE, 2, dtype=F32) / MLA_D_ROPE))
    inv_lanes = jnp.concatenate([inv, inv, jnp.zeros((LANES - 2 * half,), F32)])[None, :]
    pos = positions.astype(F32)[:, None]
    tm = 1024
    out = jax.ShapeDtypeStruct((s, LANES), F32)
    tab = pl.BlockSpec((tm, LANES), lambda i: (i, 0))
    return pl.pallas_call(
        _rope_table_kernel,
        out_shape=(out, out, out),
        grid=(s // tm,),
        in_specs=[pl.BlockSpec((tm, 1), lambda i: (i, 0)),
                  pl.BlockSpec((1, LANES), lambda i: (0, 0))],
        out_specs=(tab, tab, tab),
        compiler_params=_params("arbitrary"),
        name="rope_tables",
    )(pos, inv_lanes)


def _rotate_lanes(r, c, s1, s2):
    return r * c + pltpu.roll(r, 96, axis=1) * s1 + pltpu.roll(r, 32, axis=1) * s2


def _mla_down_kernel(a_ref, wq_ref, wkv_ref, qn_ref, kvn_ref, c_ref, s1_ref, s2_ref,
                     cq_ref, ckv_ref, kr_ref, accq_ref, acckv_ref, *, nk):
    k = pl.program_id(1)
    a = a_ref[...]
    pq = jnp.dot(a, wq_ref[...], preferred_element_type=F32)
    pkv = jnp.dot(a, wkv_ref[...], preferred_element_type=F32)

    @pl.when(k == 0)
    def _():
        accq_ref[...] = pq
        acckv_ref[...] = pkv

    @pl.when(k > 0)
    def _():
        accq_ref[...] += pq
        acckv_ref[...] += pkv

    @pl.when(k == nk - 1)
    def _():
        cq = accq_ref[...]
        ms = jnp.mean(cq * cq, axis=-1, keepdims=True)
        cq_ref[...] = (cq * lax.rsqrt(ms + RMS_EPS) * qn_ref[...]).astype(cq_ref.dtype)
        ckv = acckv_ref[:, :MLA_KV_LORA]
        ms = jnp.mean(ckv * ckv, axis=-1, keepdims=True)
        ckv_ref[...] = (ckv * lax.rsqrt(ms + RMS_EPS) * kvn_ref[...]).astype(ckv_ref.dtype)
        kr = acckv_ref[:, MLA_KV_LORA:]
        kr_ref[...] = _rotate_lanes(kr, c_ref[...], s1_ref[...], s2_ref[...]).astype(kr_ref.dtype)


def _mla_down(h, w_dq, w_dkv_pad, q_norm, kv_norm, rope_c, rope_s1, rope_s2):
    m, kdim = h.shape
    tm, tk = 1024, 1024
    nk = kdim // tk
    nq = w_dq.shape[1]
    nkv = w_dkv_pad.shape[1]
    tab = pl.BlockSpec((tm, LANES), lambda i, k: (i, 0))
    return pl.pallas_call(
        functools.partial(_mla_down_kernel, nk=nk),
        out_shape=(jax.ShapeDtypeStruct((m, nq), BF16),
                   jax.ShapeDtypeStruct((m, MLA_KV_LORA), BF16),
                   jax.ShapeDtypeStruct((m, LANES), BF16)),
        grid=(m // tm, nk),
        in_specs=[
            pl.BlockSpec((tm, tk), lambda i, k: (i, k)),
            pl.BlockSpec((tk, nq), lambda i, k: (k, 0)),
            pl.BlockSpec((tk, nkv), lambda i, k: (k, 0)),
            pl.BlockSpec((1, nq), lambda i, k: (0, 0)),
            pl.BlockSpec((1, MLA_KV_LORA), lambda i, k: (0, 0)),
            tab, tab, tab,
        ],
        out_specs=(pl.BlockSpec((tm, nq), lambda i, k: (i, 0)),
                   pl.BlockSpec((tm, MLA_KV_LORA), lambda i, k: (i, 0)),
                   pl.BlockSpec((tm, LANES), lambda i, k: (i, 0))),
        scratch_shapes=[pltpu.VMEM((tm, nq), F32), pltpu.VMEM((tm, nkv), F32)],
        compiler_params=_params("arbitrary", "arbitrary"),
        name="mla_down",
    )(h, w_dq, w_dkv_pad, q_norm, kv_norm, rope_c, rope_s1, rope_s2)


def _mla_q_up_kernel(a_ref, w_ref, c_ref, s1_ref, s2_ref, o_ref, *, heads_per_tile):
    acc = jnp.dot(a_ref[...], w_ref[...], preferred_element_type=F32)
    scale = (MLA_D_NOPE + MLA_D_ROPE) ** -0.5 * LOG2_E
    c, s1, s2 = c_ref[...], s1_ref[...], s2_ref[...]
    for hh in range(heads_per_tile):
        lo = hh * MLA_D_QK_PAD
        o_ref[:, lo:lo + LANES] = (acc[:, lo:lo + LANES] * scale).astype(o_ref.dtype)
        r = _rotate_lanes(acc[:, lo + LANES:lo + 2 * LANES], c, s1, s2)
        o_ref[:, lo + LANES:lo + 2 * LANES] = (r * scale).astype(o_ref.dtype)


def _mla_q_up(cq, w_uq_pad, rope_c, rope_s1, rope_s2):
    m, kdim = cq.shape
    n = w_uq_pad.shape[1]
    tm, tn = 512, 1024
    tab = pl.BlockSpec((tm, LANES), lambda i, j: (i, 0))
    return pl.pallas_call(
        functools.partial(_mla_q_up_kernel, heads_per_tile=tn // MLA_D_QK_PAD),
        out_shape=jax.ShapeDtypeStruct((m, n), BF16),
        grid=(m // tm, n // tn),
        in_specs=[pl.BlockSpec((tm, kdim), lambda i, j: (i, 0)),
                  pl.BlockSpec((kdim, tn), lambda i, j: (0, j)),
                  tab, tab, tab],
        out_specs=pl.BlockSpec((tm, tn), lambda i, j: (i, j)),
        compiler_params=_params("arbitrary", "arbitrary"),
        name="mla_q_up",
    )(cq, w_uq_pad, rope_c, rope_s1, rope_s2)


def _mla_kv_up_kernel(a_ref, wk_ref, wv_ref, kr_ref, k_ref, v_ref):
    a = a_ref[...]
    kn = jnp.dot(a, wk_ref[...], preferred_element_type=F32)
    v_ref[...] = jnp.dot(a, wv_ref[...], preferred_element_type=F32).astype(v_ref.dtype)
    kr = kr_ref[...]
    for hh in range(MLA_HEADS):
        lo = hh * MLA_D_QK_PAD
        k_ref[:, lo:lo + LANES] = kn[:, hh * LANES:(hh + 1) * LANES].astype(k_ref.dtype)
        k_ref[:, lo + LANES:lo + 2 * LANES] = kr


def _mla_kv_up(ckv, w_uk, w_uv, k_rope):
    m, kdim = ckv.shape
    tm = 512
    nk = w_uk.shape[1]
    nv = w_uv.shape[1]
    return pl.pallas_call(
        _mla_kv_up_kernel,
        out_shape=(jax.ShapeDtypeStruct((m, MLA_HEADS * MLA_D_QK_PAD), BF16),
                   jax.ShapeDtypeStruct((m, nv), BF16)),
        grid=(m // tm,),
        in_specs=[pl.BlockSpec((tm, kdim), lambda i: (i, 0)),
                  pl.BlockSpec((kdim, nk), lambda i: (0, 0)),
                  pl.BlockSpec((kdim, nv), lambda i: (0, 0)),
                  pl.BlockSpec((tm, LANES), lambda i: (i, 0))],
        out_specs=(pl.BlockSpec((tm, MLA_HEADS * MLA_D_QK_PAD), lambda i: (i, 0)),
                   pl.BlockSpec((tm, nv), lambda i: (i, 0))),
        compiler_params=_params("arbitrary"),
        name="mla_kv_up",
    )(ckv, w_uk, w_uv, k_rope)


def _qk(q, k):
    return lax.dot_general(q, k, (((1,), (1,)), ((), ())), preferred_element_type=F32)


def _mla_attn_kernel(q_ref, k_ref, v_ref, o_ref, *, tb, nsub):
    i = pl.program_id(1)
    first = i * nsub
    lower = (lax.broadcasted_iota(jnp.int32, (tb, tb), 1)
             <= lax.broadcasted_iota(jnp.int32, (tb, tb), 0))

    def steps(chains, sts, masked):
        sts = list(sts)
        ss = {}
        for r, j in chains:
            start = pl.multiple_of(j * tb, tb)
            s = _qk(q_ref[r * tb:(r + 1) * tb, :], k_ref[pl.ds(start, tb), :])
            ss[r] = jnp.where(lower, s, -jnp.inf) if masked else s
        m_new = {r: jnp.maximum(sts[r][0], jnp.max(ss[r], axis=-1, keepdims=True))
                 for r, _ in chains}
        for r, j in chains:
            start = pl.multiple_of(j * tb, tb)
            m, l, acc = sts[r]
            alpha = jnp.exp2(m - m_new[r])
            p = jnp.exp2(ss[r] - m_new[r])
            l = alpha * l + jnp.sum(p, axis=-1, keepdims=True)
            pv = jnp.dot(p.astype(BF16), v_ref[pl.ds(start, tb), :],
                         preferred_element_type=F32)
            sts[r] = (m_new[r], l, alpha * acc + pv)
        return tuple(sts)

    states = tuple((jnp.full((tb, 1), -jnp.inf, F32), jnp.zeros((tb, 1), F32),
                    jnp.zeros((tb, MLA_D_V), F32)) for _ in range(nsub))
    for d in range(nsub):
        states = steps([(r, first + r - d) for r in range(d, nsub)], states, d == 0)

    def body(j, sts):
        return steps([(r, j) for r in range(nsub)], sts, False)

    states = lax.fori_loop(0, first, body, states)
    for r in range(nsub):
        m, l, acc = states[r]
        o_ref[r * tb:(r + 1) * tb, :] = (acc / l).astype(o_ref.dtype)


def _mla_attention(q, k, v, *, tb, nsub):
    s = q.shape[0]
    tq = tb * nsub
    return pl.pallas_call(
        functools.partial(_mla_attn_kernel, tb=tb, nsub=nsub),
        out_shape=jax.ShapeDtypeStruct((s, MLA_HEADS * MLA_D_V), BF16),
        grid=(MLA_HEADS, s // tq),
        in_specs=[pl.BlockSpec((tq, MLA_D_QK_PAD), lambda h, i: (i, h)),
                  pl.BlockSpec((s, MLA_D_QK_PAD), lambda h, i: (0, h)),
                  pl.BlockSpec((s, MLA_D_V), lambda h, i: (0, h))],
        out_specs=pl.BlockSpec((tq, MLA_D_V), lambda h, i: (i, h)),
        compiler_params=_params("arbitrary", "arbitrary"),
        name="mla_attention",
    )(q, k, v)


def _sb_attn_kernel(q_ref, k_ref, v_ref, o_ref, *, tb, nsub):
    i = pl.program_id(1)
    first = i * nsub
    strict = (lax.broadcasted_iota(jnp.int32, (tb, tb), 1)
              < lax.broadcasted_iota(jnp.int32, (tb, tb), 0))
    tri2 = ((lax.broadcasted_iota(jnp.int32, (2 * tb, tb), 0) & (tb - 1))
            >= lax.broadcasted_iota(jnp.int32, (2 * tb, tb), 1)).astype(BF16)

    def steps(chains, sts, masked):
        sts = list(sts)
        zs, sps, hls, cums, as_ = {}, {}, {}, {}, {}

        def qk(r, j):
            zs[r] = _qk(q_ref[r * tb:(r + 1) * tb, :],
                        k_ref[pl.ds(pl.multiple_of(j * tb, tb), tb), :])

        def softplus(r, j):
            z = zs[r]
            sp = jnp.maximum(z, 0.0) + jnp.log2(1.0 + jnp.exp2(-jnp.abs(z)))
            sp = jnp.where(strict, sp, 0.0) if masked else sp
            hi = sp.astype(BF16)
            lo = (sp - hi.astype(F32)).astype(BF16)
            sps[r] = sp
            hls[r] = jnp.concatenate([hi, lo], axis=1)

        def cumsum(r, j):
            cums[r] = jnp.dot(hls[r], tri2, preferred_element_type=F32)

        def weights(r, j):
            a = jnp.exp2(zs[r] - (cums[r] + sts[r][0]))
            as_[r] = (jnp.where(strict, a, 0.0) if masked else a).astype(BF16)

        def av(r, j):
            pv = jnp.dot(as_[r], v_ref[pl.ds(pl.multiple_of(j * tb, tb), tb), :],
                         preferred_element_type=F32)
            sts[r] = (sts[r][0] + jnp.sum(sps[r], axis=-1, keepdims=True), sts[r][1] + pv)

        for stage in (qk, softplus, cumsum, weights, av):
            for r, j in chains:
                stage(r, j)
        return tuple(sts)

    states = tuple((jnp.zeros((tb, 1), F32), jnp.zeros((tb, SB_HEAD_DIM), F32))
                   for _ in range(nsub))
    for d in range(nsub):
        states = steps([(r, first + r - d) for r in range(d, nsub)], states, d == 0)

    def body(jj, sts):
        j = first - 1 - jj
        return steps([(r, j) for r in range(nsub)], sts, False)

    states = lax.fori_loop(0, first, body, states)
    for r in range(nsub):
        o_ref[r * tb:(r + 1) * tb, :] = states[r][1].astype(o_ref.dtype)


def _sb_attention(qkv, *, tb, nsub):
    s = qkv.shape[0]
    hd = SB_HEAD_DIM
    tq = tb * nsub
    return pl.pallas_call(
        functools.partial(_sb_attn_kernel, tb=tb, nsub=nsub),
        out_shape=jax.ShapeDtypeStruct((s, SB_HEADS * hd), BF16),
        grid=(SB_HEADS, s // tq),
        in_specs=[pl.BlockSpec((tq, hd), lambda h, i: (i, h)),
                  pl.BlockSpec((s, hd), lambda h, i: (0, SB_HEADS + h)),
                  pl.BlockSpec((s, hd), lambda h, i: (0, 2 * SB_HEADS + h))],
        out_specs=pl.BlockSpec((tq, hd), lambda h, i: (i, h)),
        compiler_params=_params("arbitrary", "arbitrary"),
        name="sb_attention",
    )(qkv, qkv, qkv)


def _pool_kernel(x_ref, halo_ref, g_ref, sh_ref, sc_ref, w_ref, ps_ref, gate_ref,
                 o_ref, hbuf_ref, *, tm):
    i = pl.program_id(0)
    g, sh, sc = g_ref[...], sh_ref[...], sc_ref[...]
    x = x_ref[...]
    h = _modulated(x, g, sh, sc)
    h_halo = _modulated(halo_ref[...], g, sh, sc)
    hbuf_ref[:POOL_HALO, :] = jnp.where(i > 0, h_halo, 0.0)
    hbuf_ref[POOL_HALO:, :] = h
    t = i * tm + lax.broadcasted_iota(jnp.int32, (tm, 1), 0)
    for gi, w in enumerate(POOL_WINDOWS):
        lo, hi = gi * POOL_GROUP, (gi + 1) * POOL_GROUP
        ssum = h[:, lo:hi]
        for k in range(1, w):
            ssum = ssum + hbuf_ref[POOL_HALO - k:POOL_HALO - k + tm, lo:hi]
        cnt = jnp.minimum(t + 1, w).astype(F32)
        p = ssum / cnt - h[:, lo:hi]
        y = jnp.dot(p.astype(BF16), w_ref[gi], preferred_element_type=F32)
        o_ref[:, lo:hi] = x[:, lo:hi] + gate_ref[:, lo:hi] * (y * ps_ref[:, lo:hi])


def _pool_layer(x, gain, shift, scale, w_pool, pool_scale, gate):
    s, d = x.shape
    tm = 256
    ng = len(POOL_WINDOWS)
    vec = pl.BlockSpec((1, d), lambda i: (0, 0))
    halo_blocks = tm // POOL_HALO
    return pl.pallas_call(
        functools.partial(_pool_kernel, tm=tm),
        out_shape=jax.ShapeDtypeStruct((s, d), F32),
        grid=(s // tm,),
        in_specs=[pl.BlockSpec((tm, d), lambda i: (i, 0)),
                  pl.BlockSpec((POOL_HALO, d),
                               lambda i: (jnp.maximum(i * halo_blocks - 1, 0), 0)),
                  vec, vec, vec,
                  pl.BlockSpec((ng, POOL_GROUP, POOL_GROUP), lambda i: (0, 0, 0)),
                  vec, vec],
        out_specs=pl.BlockSpec((tm, d), lambda i: (i, 0)),
        scratch_shapes=[pltpu.VMEM((tm + POOL_HALO, d), F32)],
        compiler_params=_params("arbitrary"),
        name="pool_layer",
    )(x, x, gain, shift, scale, w_pool, pool_scale, gate)


def _pad_cols(w, n):
    return jnp.pad(w, ((0, 0), (0, n - w.shape[1])))


def _mla_weights(w_dq, w_uq, w_dkv, w_ukv, w_o):
    h = MLA_HEADS
    w_uq_h = w_uq.reshape(MLA_Q_LORA, h, MLA_D_NOPE + MLA_D_ROPE)
    w_uq_pad = jnp.pad(w_uq_h, ((0, 0), (0, 0), (0, MLA_D_QK_PAD - MLA_D_NOPE - MLA_D_ROPE)))
    w_uq_pad = w_uq_pad.reshape(MLA_Q_LORA, h * MLA_D_QK_PAD).astype(BF16)
    w_ukv_h = w_ukv.reshape(MLA_KV_LORA, h, MLA_D_NOPE + MLA_D_V)
    w_uk = w_ukv_h[:, :, :MLA_D_NOPE].reshape(MLA_KV_LORA, h * MLA_D_NOPE).astype(BF16)
    w_uv = w_ukv_h[:, :, MLA_D_NOPE:].reshape(MLA_KV_LORA, h * MLA_D_V).astype(BF16)
    w_dkv_pad = _pad_cols(w_dkv, MLA_KVA_PAD).astype(BF16)
    return w_dq.astype(BF16), w_uq_pad, w_dkv_pad, w_uk, w_uv, w_o.astype(BF16)


def _mla_layer(x, h, rope, w_dq, q_norm, w_uq, w_dkv, kv_norm, w_ukv, w_o, gate):
    w_dq_b, w_uq_pad, w_dkv_pad, w_uk, w_uv, w_o_b = _mla_weights(w_dq, w_uq, w_dkv, w_ukv, w_o)
    cq, ckv, k_rope = _mla_down(h, w_dq_b, w_dkv_pad, q_norm[None, :], kv_norm[None, :], *rope)
    q = _mla_q_up(cq, w_uq_pad, *rope)
    k, v = _mla_kv_up(ckv, w_uk, w_uv, k_rope)
    o = _mla_attention(q, k, v, tb=512, nsub=2)
    return _mm_resid(o, w_o_b, x, gate, tm=1024, tn=1024, tk=o.shape[1], name="mla_out")


def _sb_layer(x, h, w_qkv, w_o, gate):
    n = w_qkv.shape[1]
    colscale = jnp.where(jnp.arange(n) < SB_HEADS * SB_HEAD_DIM,
                         SB_HEAD_DIM ** -0.5 * LOG2_E, 1.0).astype(F32)[None, :]
    qkv = _mm_scale(h, w_qkv.astype(BF16), colscale, tm=1024, tn=1024, name="sb_qkv")
    o = _sb_attention(qkv, tb=256, nsub=8)
    return _mm_resid(o, w_o.astype(BF16), x, gate, tm=1024, tn=1024, tk=o.shape[1],
                     name="sb_out")


def _ffn_weights(w_gate, w_up, w_down):
    pad = D_FF_PAD - D_FF
    wg = jnp.pad(w_gate.astype(BF16), ((0, 0), (0, 0), (0, pad)))
    wu = jnp.pad(w_up.astype(BF16), ((0, 0), (0, 0), (0, pad)))
    wd = jnp.pad(w_down.astype(BF16), ((0, 0), (0, pad), (0, 0)))
    return wg, wu, wd


def _ffn_layer(x, h, ffn_w, layer, gate):
    wg, wu, wd = ffn_w
    act = _ffn_up(h, wg, wu, layer, tm=1024, tn=512)
    return _mm_resid(act, wd, x, gate, tm=1024, tn=1024, tk=2816, name="ffn_down",
                     layer=layer)


def kernel(x, c, positions, ada_w, ada_table, norm_mix, norm_ffn, norm_final,
           mla_w_dq, mla_q_norm, mla_w_uq, mla_w_dkv, mla_kv_norm, mla_w_ukv, mla_w_o,
           sb_w_qkv, sb_w_o, pool_w, pool_scale, ffn_w_gate, ffn_w_up, ffn_w_down):
    b, s, d = x.shape
    assert b == 1 and c.shape == (1, d)
    xs = x.reshape(s, d)
    mod = _ada_mod(c, ada_w, ada_table).reshape(DEPTH, N_MOD, 1, d)
    rope = _rope_tables(positions.reshape(s))
    ffn_w = _ffn_weights(ffn_w_gate, ffn_w_up, ffn_w_down)
    for i in range(DEPTH):
        shift_m, scale_m, gate_m = mod[i, 0], mod[i, 1], mod[i, 2]
        shift_f, scale_f, gate_f = mod[i, 3], mod[i, 4], mod[i, 5]
        gain_m = norm_mix[i][None, :]
        kind, j = i % N_MIXERS, i // N_MIXERS
        if kind == 0:
            h = _modulate(xs, gain_m, shift_m, scale_m)
            xs = _mla_layer(xs, h, rope, mla_w_dq[j], mla_q_norm[j], mla_w_uq[j],
                            mla_w_dkv[j], mla_kv_norm[j], mla_w_ukv[j], mla_w_o[j], gate_m)
        elif kind == 1:
            h = _modulate(xs, gain_m, shift_m, scale_m)
            xs = _sb_layer(xs, h, sb_w_qkv[j], sb_w_o[j], gate_m)
        else:
            xs = _pool_layer(xs, gain_m, shift_m, scale_m, pool_w[j].astype(BF16),
                             pool_scale[j][None, :], gate_m)
        h = _modulate(xs, norm_ffn[i][None, :], shift_f, scale_f)
        xs = _ffn_layer(xs, h, ffn_w, i, gate_f)
    return _final_norm(xs, norm_final[None, :]).reshape(b, s, d)
```

```python
import functools

import jax
import jax.numpy as jnp
from jax import lax
from jax.experimental import pallas as pl
from jax.experimental.pallas import tpu as pltpu

F32 = jnp.float32
BF16 = jnp.bfloat16

D_MODEL = 4096
SEQ = 16384
DEPTH = 4
N_MIXERS = 3
RMS_EPS = 1e-6
N_MOD = 6

MLA_HEADS = 16
MLA_Q_LORA = 1536
MLA_KV_LORA = 512
MLA_D_NOPE = 128
MLA_D_ROPE = 64
MLA_D_V = 128
ROPE_THETA = 10000.0
MLA_D_QK_PAD = 256
MLA_KVA_PAD = MLA_KV_LORA + 128

SB_HEADS = 16
SB_HEAD_DIM = 128

POOL_WINDOWS = (2, 4, 8, 16)
POOL_GROUP = D_MODEL // len(POOL_WINDOWS)
POOL_HALO = 16

D_FF = ((8 * D_MODEL + 3 * 256 - 1) // (3 * 256)) * 256
D_FF_PAD = 11264

LANES = 128
LOG2_E = 1.4426950408889634
VMEM_LIMIT = 56 * 1024 * 1024


def _params(*sem):
    return pltpu.CompilerParams(dimension_semantics=sem, vmem_limit_bytes=VMEM_LIMIT)


def _ada_kernel(c_ref, w_ref, tab_ref, o_ref):
    c = c_ref[...]
    s = c * jax.nn.sigmoid(c)
    lhs = jnp.broadcast_to(s, (8, s.shape[1])).astype(BF16)
    base = jnp.dot(lhs, w_ref[...].astype(BF16), preferred_element_type=F32)
    o_ref[...] = base[0:1, :] + tab_ref[...]


def _ada_mod(c, ada_w, ada_table):
    d = c.shape[1]
    n = ada_w.shape[1]
    tn = 512
    tab = ada_table.reshape(DEPTH, n)
    return pl.pallas_call(
        _ada_kernel,
        out_shape=jax.ShapeDtypeStruct((DEPTH, n), F32),
        grid=(n // tn,),
        in_specs=[
            pl.BlockSpec((1, d), lambda j: (0, 0)),
            pl.BlockSpec((d, tn), lambda j: (0, j)),
            pl.BlockSpec((DEPTH, tn), lambda j: (0, j)),
        ],
        out_specs=pl.BlockSpec((DEPTH, tn), lambda j: (0, j)),
        compiler_params=_params("arbitrary"),
        name="ada_mod",
    )(c, ada_w, tab)


def _modulated(x, gain, shift, scale):
    ms = jnp.mean(x * x, axis=-1, keepdims=True)
    y = x * lax.rsqrt(ms + RMS_EPS) * gain
    return y * (1.0 + scale) + shift


def _modulate_kernel(x_ref, g_ref, sh_ref, sc_ref, o_ref):
    h = _modulated(x_ref[...], g_ref[...], sh_ref[...], sc_ref[...])
    o_ref[...] = h.astype(o_ref.dtype)


def _modulate(x, gain, shift, scale):
    s, d = x.shape
    tm = 512
    vec = pl.BlockSpec((1, d), lambda i: (0, 0))
    return pl.pallas_call(
        _modulate_kernel,
        out_shape=jax.ShapeDtypeStruct((s, d), BF16),
        grid=(s // tm,),
        in_specs=[pl.BlockSpec((tm, d), lambda i: (i, 0)), vec, vec, vec],
        out_specs=pl.BlockSpec((tm, d), lambda i: (i, 0)),
        compiler_params=_params("arbitrary"),
        name="modulate",
    )(x, gain, shift, scale)


def _final_norm_kernel(x_ref, g_ref, o_ref):
    x = x_ref[...]
    ms = jnp.mean(x * x, axis=-1, keepdims=True)
    o_ref[...] = x * lax.rsqrt(ms + RMS_EPS) * g_ref[...]


def _final_norm(x, gain):
    s, d = x.shape
    tm = 256
    return pl.pallas_call(
        _final_norm_kernel,
        out_shape=jax.ShapeDtypeStruct((s, d), F32),
        grid=(s // tm,),
        in_specs=[pl.BlockSpec((tm, d), lambda i: (i, 0)),
                  pl.BlockSpec((1, d), lambda i: (0, 0))],
        out_specs=pl.BlockSpec((tm, d), lambda i: (i, 0)),
        compiler_params=_params("arbitrary"),
        name="final_norm",
    )(x, gain)


def _mm_resid_kernel(a_ref, w_ref, x_ref, g_ref, o_ref, acc_ref, *, nk):
    k = pl.program_id(2)
    part = jnp.dot(a_ref[...], w_ref[...], preferred_element_type=F32)
    if nk == 1:
        o_ref[...] = x_ref[...] + g_ref[...] * part
        return

    @pl.when(k == 0)
    def _():
        acc_ref[...] = part

    @pl.when(jnp.logical_and(k > 0, k < nk - 1))
    def _():
        acc_ref[...] += part

    @pl.when(k == nk - 1)
    def _():
        o_ref[...] = x_ref[...] + g_ref[...] * (acc_ref[...] + part)


def _mm_resid(a, w, x, gate, *, tm, tn, tk, name, layer=None):
    m, kdim = a.shape
    n = w.shape[-1]
    nk = kdim // tk
    if layer is None:
        wspec = pl.BlockSpec((tk, tn), lambda i, j, k: (k, j))
    else:
        wspec = pl.BlockSpec((None, tk, tn), lambda i, j, k: (layer, k, j))
    return pl.pallas_call(
        functools.partial(_mm_resid_kernel, nk=nk),
        out_shape=jax.ShapeDtypeStruct((m, n), F32),
        grid=(m // tm, n // tn, nk),
        in_specs=[
            pl.BlockSpec((tm, tk), lambda i, j, k: (i, k)),
            wspec,
            pl.BlockSpec((tm, tn), lambda i, j, k: (i, j)),
            pl.BlockSpec((1, tn), lambda i, j, k: (0, j)),
        ],
        out_specs=pl.BlockSpec((tm, tn), lambda i, j, k: (i, j)),
        scratch_shapes=[pltpu.VMEM((tm, tn), F32)],
        compiler_params=_params("arbitrary", "arbitrary", "arbitrary"),
        name=name,
    )(a, w, x, gate)


def _mm_scale_kernel(a_ref, w_ref, s_ref, o_ref):
    acc = jnp.dot(a_ref[...], w_ref[...], preferred_element_type=F32)
    o_ref[...] = (acc * s_ref[...]).astype(o_ref.dtype)


def _mm_scale(a, w, colscale, *, tm, tn, name):
    m, kdim = a.shape
    n = w.shape[1]
    return pl.pallas_call(
        _mm_scale_kernel,
        out_shape=jax.ShapeDtypeStruct((m, n), BF16),
        grid=(m // tm, n // tn),
        in_specs=[
            pl.BlockSpec((tm, kdim), lambda i, j: (i, 0)),
            pl.BlockSpec((kdim, tn), lambda i, j: (0, j)),
            pl.BlockSpec((1, tn), lambda i, j: (0, j)),
        ],
        out_specs=pl.BlockSpec((tm, tn), lambda i, j: (i, j)),
        compiler_params=_params("arbitrary", "arbitrary"),
        name=name,
    )(a, w, colscale)


def _ffn_up_kernel(a_ref, wg_ref, wu_ref, o_ref):
    a = a_ref[...]
    g = jnp.dot(a, wg_ref[...], preferred_element_type=F32)
    u = jnp.dot(a, wu_ref[...], preferred_element_type=F32)
    o_ref[...] = (g * jax.nn.sigmoid(g) * u).astype(o_ref.dtype)


def _ffn_up(h, wg, wu, layer, *, tm, tn):
    m, kdim = h.shape
    n = wg.shape[-1]
    wspec = pl.BlockSpec((None, kdim, tn), lambda i, j: (layer, 0, j))
    return pl.pallas_call(
        _ffn_up_kernel,
        out_shape=jax.ShapeDtypeStruct((m, n), BF16),
        grid=(m // tm, n // tn),
        in_specs=[pl.BlockSpec((tm, kdim), lambda i, j: (i, 0)), wspec, wspec],
        out_specs=pl.BlockSpec((tm, tn), lambda i, j: (i, j)),
        compiler_params=_params("arbitrary", "arbitrary"),
        name="ffn_up",
    )(h, wg, wu)


def _rope_table_kernel(pos_ref, inv_ref, c_ref, s1_ref, s2_ref):
    ang = pos_ref[...] * inv_ref[...]
    cos = jnp.cos(ang)
    sin = jnp.sin(ang)
    lane = lax.broadcasted_iota(jnp.int32, ang.shape, 1)
    half = MLA_D_ROPE // 2
    c_ref[...] = jnp.where(lane < MLA_D_ROPE, cos, 0.0)
    s1_ref[...] = jnp.where(lane < half, -sin, 0.0)
    s2_ref[...] = jnp.where(jnp.logical_and(lane >= half, lane < MLA_D_ROPE), sin, 0.0)


def _rope_tables(positions):
    s = positions.shape[0]
    half = MLA_D_ROPE // 2
    inv = 1.0 / (ROPE_THETA ** (jnp.arange(0, MLA_D_ROPE, 2, dtype=F32) / MLA_D_ROPE))
    inv_lanes = jnp.concatenate([inv, inv, jnp.zeros((LANES - 2 * half,), F32)])[None, :]
    pos = positions.astype(F32)[:, None]
    tm = 1024
    out = jax.ShapeDtypeStruct((s, LANES), F32)
    tab = pl.BlockSpec((tm, LANES), lambda i: (i, 0))
    return pl.pallas_call(
        _rope_table_kernel,
        out_shape=(out, out, out),
        grid=(s // tm,),
        in_specs=[pl.BlockSpec((tm, 1), lambda i: (i, 0)),
                  pl.BlockSpec((1, LANES), lambda i: (0, 0))],
        out_specs=(tab, tab, tab),
        compiler_params=_params("arbitrary"),
        name="rope_tables",
    )(pos, inv_lanes)


def _rotate_lanes(r, c, s1, s2):
    return r * c + pltpu.roll(r, 96, axis=1) * s1 + pltpu.roll(r, 32, axis=1) * s2


def _mla_down_kernel(a_ref, wq_ref, wkv_ref, qn_ref, kvn_ref, c_ref, s1_ref, s2_ref,
                     cq_ref, ckv_ref, kr_ref, accq_ref, acckv_ref, *, nk):
    k = pl.program_id(1)
    a = a_ref[...]
    pq = jnp.dot(a, wq_ref[...], preferred_element_type=F32)
    pkv = jnp.dot(a, wkv_ref[...], preferred_element_type=F32)

    @pl.when(k == 0)
    def _():
        accq_ref[...] = pq
        acckv_ref[...] = pkv

    @pl.when(k > 0)
    def _():
        accq_ref[...] += pq
        acckv_ref[...] += pkv

    @pl.when(k == nk - 1)
    def _():
        cq = accq_ref[...]
        ms = jnp.mean(cq * cq, axis=-1, keepdims=True)
        cq_ref[...] = (cq * lax.rsqrt(ms + RMS_EPS) * qn_ref[...]).astype(cq_ref.dtype)
        ckv = acckv_ref[:, :MLA_KV_LORA]
        ms = jnp.mean(ckv * ckv, axis=-1, keepdims=True)
        ckv_ref[...] = (ckv * lax.rsqrt(ms + RMS_EPS) * kvn_ref[...]).astype(ckv_ref.dtype)
        kr = acckv_ref[:, MLA_KV_LORA:]
        kr_ref[...] = _rotate_lanes(kr, c_ref[...], s1_ref[...], s2_ref[...]).astype(kr_ref.dtype)


def _mla_down(h, w_dq, w_dkv_pad, q_norm, kv_norm, rope_c, rope_s1, rope_s2):
    m, kdim = h.shape
    tm, tk = 1024, 1024
    nk = kdim // tk
    nq = w_dq.shape[1]
    nkv = w_dkv_pad.shape[1]
    tab = pl.BlockSpec((tm, LANES), lambda i, k: (i, 0))
    return pl.pallas_call(
        functools.partial(_mla_down_kernel, nk=nk),
        out_shape=(jax.ShapeDtypeStruct((m, nq), BF16),
                   jax.ShapeDtypeStruct((m, MLA_KV_LORA), BF16),
                   jax.ShapeDtypeStruct((m, LANES), BF16)),
        grid=(m // tm, nk),
        in_specs=[
            pl.BlockSpec((tm, tk), lambda i, k: (i, k)),
            pl.BlockSpec((tk, nq), lambda i, k: (k, 0)),
            pl.BlockSpec((tk, nkv), lambda i, k: (k, 0)),
            pl.BlockSpec((1, nq), lambda i, k: (0, 0)),
            pl.BlockSpec((1, MLA_KV_LORA), lambda i, k: (0, 0)),
            tab, tab, tab,
        ],
        out_specs=(pl.BlockSpec((tm, nq), lambda i, k: (i, 0)),
                   pl.BlockSpec((tm, MLA_KV_LORA), lambda i, k: (i, 0)),
                   pl.BlockSpec((tm, LANES), lambda i, k: (i, 0))),
        scratch_shapes=[pltpu.VMEM((tm, nq), F32), pltpu.VMEM((tm, nkv), F32)],
        compiler_params=_params("arbitrary", "arbitrary"),
        name="mla_down",
    )(h, w_dq, w_dkv_pad, q_norm, kv_norm, rope_c, rope_s1, rope_s2)


def _mla_q_up_kernel(a_ref, w_ref, c_ref, s1_ref, s2_ref, o_ref, *, heads_per_tile):
    acc = jnp.dot(a_ref[...], w_ref[...], preferred_element_type=F32)
    scale = (MLA_D_NOPE + MLA_D_ROPE) ** -0.5 * LOG2_E
    c, s1, s2 = c_ref[...], s1_ref[...], s2_ref[...]
    for hh in range(heads_per_tile):
        lo = hh * MLA_D_QK_PAD
        o_ref[:, lo:lo + LANES] = (acc[:, lo:lo + LANES] * scale).astype(o_ref.dtype)
        r = _rotate_lanes(acc[:, lo + LANES:lo + 2 * LANES], c, s1, s2)
        o_ref[:, lo + LANES:lo + 2 * LANES] = (r * scale).astype(o_ref.dtype)


def _mla_q_up(cq, w_uq_pad, rope_c, rope_s1, rope_s2):
    m, kdim = cq.shape
    n = w_uq_pad.shape[1]
    tm, tn = 512, 1024
    tab = pl.BlockSpec((tm, LANES), lambda i, j: (i, 0))
    return pl.pallas_call(
        functools.partial(_mla_q_up_kernel, heads_per_tile=tn // MLA_D_QK_PAD),
        out_shape=jax.ShapeDtypeStruct((m, n), BF16),
        grid=(m // tm, n // tn),
        in_specs=[pl.BlockSpec((tm, kdim), lambda i, j: (i, 0)),
                  pl.BlockSpec((kdim, tn), lambda i, j: (0, j)),
                  tab, tab, tab],
        out_specs=pl.BlockSpec((tm, tn), lambda i, j: (i, j)),
        compiler_params=_params("arbitrary", "arbitrary"),
        name="mla_q_up",
    )(cq, w_uq_pad, rope_c, rope_s1, rope_s2)


def _mla_kv_up_kernel(a_ref, wk_ref, wv_ref, kr_ref, k_ref, v_ref):
    a = a_ref[...]
    kn = jnp.dot(a, wk_ref[...], preferred_element_type=F32)
    v_ref[...] = jnp.dot(a, wv_ref[...], preferred_element_type=F32).astype(v_ref.dtype)
    kr = kr_ref[...]
    for hh in range(MLA_HEADS):
        lo = hh * MLA_D_QK_PAD
        k_ref[:, lo:lo + LANES] = kn[:, hh * LANES:(hh + 1) * LANES].astype(k_ref.dtype)
        k_ref[:, lo + LANES:lo + 2 * LANES] = kr


def _mla_kv_up(ckv, w_uk, w_uv, k_rope):
    m, kdim = ckv.shape
    tm = 512
    nk = w_uk.shape[1]
    nv = w_uv.shape[1]
    return pl.pallas_call(
        _mla_kv_up_kernel,
        out_shape=(jax.ShapeDtypeStruct((m, MLA_HEADS * MLA_D_QK_PAD), BF16),
                   jax.ShapeDtypeStruct((m, nv), BF16)),
        grid=(m // tm,),
        in_specs=[pl.BlockSpec((tm, kdim), lambda i: (i, 0)),
                  pl.BlockSpec((kdim, nk), lambda i: (0, 0)),
                  pl.BlockSpec((kdim, nv), lambda i: (0, 0)),
                  pl.BlockSpec((tm, LANES), lambda i: (i, 0))],
        out_specs=(pl.BlockSpec((tm, MLA_HEADS * MLA_D_QK_PAD), lambda i: (i, 0)),
                   pl.BlockSpec((tm, nv), lambda i: (i, 0))),
        compiler_params=_params("arbitrary"),
        name="mla_kv_up",
    )(ckv, w_uk, w_uv, k_rope)


def _qk(q, k):
    return lax.dot_general(q, k, (((1,), (1,)), ((), ())), preferred_element_type=F32)


def _mla_attn_kernel(q_ref, k_ref, v_ref, o_ref, *scratch, tb, nsub):
    m_ref, l_ref, acc_ref = scratch[:nsub], scratch[nsub:2 * nsub], scratch[2 * nsub:]
    i = pl.program_id(1)
    first = i * nsub
    lower = (lax.broadcasted_iota(jnp.int32, (tb, tb), 1)
             <= lax.broadcasted_iota(jnp.int32, (tb, tb), 0))

    def steps(chains, masked):
        ss, m_new = {}, {}
        for r, j in chains:
            start = pl.multiple_of(j * tb, tb)
            s = _qk(q_ref[r * tb:(r + 1) * tb, :], k_ref[pl.ds(start, tb), :])
            ss[r] = jnp.where(lower, s, -jnp.inf) if masked else s
        for r, _ in chains:
            m_new[r] = jnp.maximum(m_ref[r][...], jnp.max(ss[r], axis=-1, keepdims=True))
        for r, j in chains:
            start = pl.multiple_of(j * tb, tb)
            alpha = jnp.exp2(m_ref[r][...] - m_new[r])
            p = jnp.exp2(ss[r] - jnp.concatenate([m_new[r]] * (tb // LANES), axis=1))
            l_ref[r][...] = alpha * l_ref[r][...] + jnp.sum(p, axis=-1, keepdims=True)
            m_ref[r][...] = m_new[r]
            pv = jnp.dot(p.astype(BF16), v_ref[pl.ds(start, tb), :],
                         preferred_element_type=F32)
            acc_ref[r][...] = alpha * acc_ref[r][...] + pv

    for r in range(nsub):
        m_ref[r][...] = jnp.full((tb, LANES), -jnp.inf, F32)
        l_ref[r][...] = jnp.zeros((tb, LANES), F32)
        acc_ref[r][...] = jnp.zeros((tb, MLA_D_V), F32)
    for d in range(nsub):
        steps([(r, first + r - d) for r in range(d, nsub)], d == 0)

    def body(j, carry):
        steps([(r, j) for r in range(nsub)], False)
        return carry

    lax.fori_loop(0, first, body, 0)
    for r in range(nsub):
        o_ref[r * tb:(r + 1) * tb, :] = (acc_ref[r][...] / l_ref[r][...]).astype(o_ref.dtype)


def _mla_attention(q, k, v, *, tb, nsub):
    s = q.shape[0]
    tq = tb * nsub
    return pl.pallas_call(
        functools.partial(_mla_attn_kernel, tb=tb, nsub=nsub),
        out_shape=jax.ShapeDtypeStruct((s, MLA_HEADS * MLA_D_V), BF16),
        grid=(MLA_HEADS, s // tq),
        in_specs=[pl.BlockSpec((tq, MLA_D_QK_PAD), lambda h, i: (i, h)),
                  pl.BlockSpec((s, MLA_D_QK_PAD), lambda h, i: (0, h)),
                  pl.BlockSpec((s, MLA_D_V), lambda h, i: (0, h))],
        out_specs=pl.BlockSpec((tq, MLA_D_V), lambda h, i: (i, h)),
        scratch_shapes=([pltpu.VMEM((tb, LANES), F32)] * (2 * nsub)
                        + [pltpu.VMEM((tb, MLA_D_V), F32)] * nsub),
        compiler_params=_params("arbitrary", "arbitrary"),
        name="mla_attention",
    )(q, k, v)


def _sb_attn_kernel(q_ref, k_ref, v_ref, o_ref, carry_ref, acc_ref, *, tb, nsub):
    i = pl.program_id(1)
    first = i * nsub
    strict = (lax.broadcasted_iota(jnp.int32, (tb, tb), 1)
              < lax.broadcasted_iota(jnp.int32, (tb, tb), 0))
    tri2 = ((lax.broadcasted_iota(jnp.int32, (2 * tb, tb), 0) & (tb - 1))
            >= lax.broadcasted_iota(jnp.int32, (2 * tb, tb), 1)).astype(BF16)

    def steps(chains, masked):
        zs, hls, cums, as_ = {}, {}, {}, {}

        def qk(r, j):
            zs[r] = _qk(q_ref[r * tb:(r + 1) * tb, :],
                        k_ref[pl.ds(pl.multiple_of(j * tb, tb), tb), :])

        def softplus(r, j):
            z = zs[r]
            neg_abs = lax.bitcast_convert_type(
                lax.bitcast_convert_type(z, jnp.uint32) | jnp.uint32(0x80000000), F32)
            sp = jnp.maximum(z, 0.0) + jnp.log2(1.0 + jnp.exp2(neg_abs))
            sp = jnp.where(strict, sp, 0.0) if masked else sp
            hi = sp.astype(BF16)
            lo = (sp - hi.astype(F32)).astype(BF16)
            hls[r] = jnp.concatenate([hi, lo], axis=1)

        def cumsum(r, j):
            cums[r] = jnp.dot(hls[r], tri2, preferred_element_type=F32)

        def weights(r, j):
            a = jnp.exp2(zs[r] - (cums[r] + carry_ref[r]))
            as_[r] = (jnp.where(strict, a, 0.0) if masked else a).astype(BF16)
            carry_ref[r] += cums[r][:, 0:1]

        def av(r, j):
            acc_ref[r] += jnp.dot(as_[r], v_ref[pl.ds(pl.multiple_of(j * tb, tb), tb), :],
                                  preferred_element_type=F32)

        for stage in (qk, softplus, cumsum, weights, av):
            for r, j in chains:
                stage(r, j)

    carry_ref[...] = jnp.zeros(carry_ref.shape, F32)
    acc_ref[...] = jnp.zeros(acc_ref.shape, F32)
    for d in range(nsub):
        steps([(r, first + r - d) for r in range(d, nsub)], d == 0)

    def body(jj, c):
        steps([(r, first - 1 - jj) for r in range(nsub)], False)
        return c

    lax.fori_loop(0, first, body, 0)
    o_ref[...] = acc_ref[...].reshape(o_ref.shape).astype(o_ref.dtype)


def _sb_attention(qkv, *, tb, nsub):
    s = qkv.shape[0]
    hd = SB_HEAD_DIM
    tq = tb * nsub
    return pl.pallas_call(
        functools.partial(_sb_attn_kernel, tb=tb, nsub=nsub),
        out_shape=jax.ShapeDtypeStruct((s, SB_HEADS * hd), BF16),
        grid=(SB_HEADS, s // tq),
        in_specs=[pl.BlockSpec((tq, hd), lambda h, i: (i, h)),
                  pl.BlockSpec((s, hd), lambda h, i: (0, SB_HEADS + h)),
                  pl.BlockSpec((s, hd), lambda h, i: (0, 2 * SB_HEADS + h))],
        out_specs=pl.BlockSpec((tq, hd), lambda h, i: (i, h)),
        scratch_shapes=[pltpu.VMEM((nsub, tb, 1), F32), pltpu.VMEM((nsub, tb, hd), F32)],
        compiler_params=_params("arbitrary", "arbitrary"),
        name="sb_attention",
    )(qkv, qkv, qkv)


def _pool_kernel(x_ref, halo_ref, g_ref, sh_ref, sc_ref, w_ref, ps_ref, gate_ref,
                 o_ref, hbuf_ref, *, tm):
    i = pl.program_id(0)
    g, sh, sc = g_ref[...], sh_ref[...], sc_ref[...]
    x = x_ref[...]
    h = _modulated(x, g, sh, sc)
    h_halo = _modulated(halo_ref[...], g, sh, sc)
    hbuf_ref[:POOL_HALO, :] = jnp.where(i > 0, h_halo, 0.0)
    hbuf_ref[POOL_HALO:, :] = h
    t = i * tm + lax.broadcasted_iota(jnp.int32, (tm, 1), 0)
    for gi, w in enumerate(POOL_WINDOWS):
        lo, hi = gi * POOL_GROUP, (gi + 1) * POOL_GROUP
        ssum = h[:, lo:hi]
        for k in range(1, w):
            ssum = ssum + hbuf_ref[POOL_HALO - k:POOL_HALO - k + tm, lo:hi]
        cnt = jnp.minimum(t + 1, w).astype(F32)
        p = ssum / cnt - h[:, lo:hi]
        y = jnp.dot(p.astype(BF16), w_ref[gi], preferred_element_type=F32)
        o_ref[:, lo:hi] = x[:, lo:hi] + gate_ref[:, lo:hi] * (y * ps_ref[:, lo:hi])


def _pool_layer(x, gain, shift, scale, w_pool, pool_scale, gate):
    s, d = x.shape
    tm = 256
    ng = len(POOL_WINDOWS)
    vec = pl.BlockSpec((1, d), lambda i: (0, 0))
    halo_blocks = tm // POOL_HALO
    return pl.pallas_call(
        functools.partial(_pool_kernel, tm=tm),
        out_shape=jax.ShapeDtypeStruct((s, d), F32),
        grid=(s // tm,),
        in_specs=[pl.BlockSpec((tm, d), lambda i: (i, 0)),
                  pl.BlockSpec((POOL_HALO, d),
                               lambda i: (jnp.maximum(i * halo_blocks - 1, 0), 0)),
                  vec, vec, vec,
                  pl.BlockSpec((ng, POOL_GROUP, POOL_GROUP), lambda i: (0, 0, 0)),
                  vec, vec],
        out_specs=pl.BlockSpec((tm, d), lambda i: (i, 0)),
        scratch_shapes=[pltpu.VMEM((tm + POOL_HALO, d), F32)],
        compiler_params=_params("arbitrary"),
        name="pool_layer",
    )(x, x, gain, shift, scale, w_pool, pool_scale, gate)


def _pad_cols(w, n):
    return jnp.pad(w, ((0, 0), (0, n - w.shape[1])))


def _mla_weights(w_dq, w_uq, w_dkv, w_ukv, w_o):
    h = MLA_HEADS
    w_uq_h = w_uq.reshape(MLA_Q_LORA, h, MLA_D_NOPE + MLA_D_ROPE)
    w_uq_pad = jnp.pad(w_uq_h, ((0, 0), (0, 0), (0, MLA_D_QK_PAD - MLA_D_NOPE - MLA_D_ROPE)))
    w_uq_pad = w_uq_pad.reshape(MLA_Q_LORA, h * MLA_D_QK_PAD).astype(BF16)
    w_ukv_h = w_ukv.reshape(MLA_KV_LORA, h, MLA_D_NOPE + MLA_D_V)
    w_uk = w_ukv_h[:, :, :MLA_D_NOPE].reshape(MLA_KV_LORA, h * MLA_D_NOPE).astype(BF16)
    w_uv = w_ukv_h[:, :, MLA_D_NOPE:].reshape(MLA_KV_LORA, h * MLA_D_V).astype(BF16)
    w_dkv_pad = _pad_cols(w_dkv, MLA_KVA_PAD).astype(BF16)
    return w_dq.astype(BF16), w_uq_pad, w_dkv_pad, w_uk, w_uv, w_o.astype(BF16)


def _mla_layer(x, h, rope, w_dq, q_norm, w_uq, w_dkv, kv_norm, w_ukv, w_o, gate):
    w_dq_b, w_uq_pad, w_dkv_pad, w_uk, w_uv, w_o_b = _mla_weights(w_dq, w_uq, w_dkv, w_ukv, w_o)
    cq, ckv, k_rope = _mla_down(h, w_dq_b, w_dkv_pad, q_norm[None, :], kv_norm[None, :], *rope)
    q = _mla_q_up(cq, w_uq_pad, *rope)
    k, v = _mla_kv_up(ckv, w_uk, w_uv, k_rope)
    o = _mla_attention(q, k, v, tb=512, nsub=4)
    return _mm_resid(o, w_o_b, x, gate, tm=1024, tn=1024, tk=o.shape[1], name="mla_out")


def _sb_layer(x, h, w_qkv, w_o, gate):
    n = w_qkv.shape[1]
    colscale = jnp.where(jnp.arange(n) < SB_HEADS * SB_HEAD_DIM,
                         SB_HEAD_DIM ** -0.5 * LOG2_E, 1.0).astype(F32)[None, :]
    qkv = _mm_scale(h, w_qkv.astype(BF16), colscale, tm=1024, tn=1024, name="sb_qkv")
    o = _sb_attention(qkv, tb=256, nsub=8)
    return _mm_resid(o, w_o.astype(BF16), x, gate, tm=1024, tn=1024, tk=o.shape[1],
                     name="sb_out")


def _cast_pad_cols_kernel(w_ref, o_ref):
    n = w_ref.shape[1]
    o_ref[:, :n] = w_ref[...].astype(o_ref.dtype)
    o_ref[:, n:] = jnp.zeros((o_ref.shape[0], o_ref.shape[1] - n), o_ref.dtype)


def _cast_pad_cols(w, n_pad):
    nl, k, n = w.shape
    tr = 256
    return pl.pallas_call(
        _cast_pad_cols_kernel,
        out_shape=jax.ShapeDtypeStruct((nl, k, n_pad), BF16),
        grid=(nl, k // tr),
        in_specs=[pl.BlockSpec((None, tr, n), lambda l, i: (l, i, 0))],
        out_specs=pl.BlockSpec((None, tr, n_pad), lambda l, i: (l, i, 0)),
        compiler_params=_params("arbitrary", "arbitrary"),
        name="cast_pad_cols",
    )(w)


def _cast_pad_rows_kernel(w_ref, o_ref, *, n_in_blocks):
    i = pl.program_id(1)

    @pl.when(i < n_in_blocks)
    def _():
        o_ref[...] = w_ref[...].astype(o_ref.dtype)

    @pl.when(i >= n_in_blocks)
    def _():
        o_ref[...] = jnp.zeros(o_ref.shape, o_ref.dtype)


def _cast_pad_rows(w, k_pad):
    nl, k, n = w.shape
    tr = 256
    n_in = k // tr
    return pl.pallas_call(
        functools.partial(_cast_pad_rows_kernel, n_in_blocks=n_in),
        out_shape=jax.ShapeDtypeStruct((nl, k_pad, n), BF16),
        grid=(nl, k_pad // tr),
        in_specs=[pl.BlockSpec((None, tr, n), lambda l, i: (l, jnp.minimum(i, n_in - 1), 0))],
        out_specs=pl.BlockSpec((None, tr, n), lambda l, i: (l, i, 0)),
        compiler_params=_params("arbitrary", "arbitrary"),
        name="cast_pad_rows",
    )(w)


def _ffn_weights(w_gate, w_up, w_down):
    return (_cast_pad_cols(w_gate, D_FF_PAD), _cast_pad_cols(w_up, D_FF_PAD),
            _cast_pad_rows(w_down, D_FF_PAD))


def _ffn_layer(x, h, ffn_w, layer, gate):
    wg, wu, wd = ffn_w
    act = _ffn_up(h, wg, wu, layer, tm=1024, tn=512)
    return _mm_resid(act, wd, x, gate, tm=512, tn=512, tk=D_FF_PAD, name="ffn_down",
                     layer=layer)


def kernel(x, c, positions, ada_w, ada_table, norm_mix, norm_ffn, norm_final,
           mla_w_dq, mla_q_norm, mla_w_uq, mla_w_dkv, mla_kv_norm, mla_w_ukv, mla_w_o,
           sb_w_qkv, sb_w_o, pool_w, pool_scale, ffn_w_gate, ffn_w_up, ffn_w_down):
    b, s, d = x.shape
    assert b == 1 and c.shape == (1, d)
    xs = x.reshape(s, d)
    mod = _ada_mod(c, ada_w, ada_table).reshape(DEPTH, N_MOD, 1, d)
    rope = _rope_tables(positions.reshape(s))
    ffn_w = _ffn_weights(ffn_w_gate, ffn_w_up, ffn_w_down)
    for i in range(DEPTH):
        shift_m, scale_m, gate_m = mod[i, 0], mod[i, 1], mod[i, 2]
        shift_f, scale_f, gate_f = mod[i, 3], mod[i, 4], mod[i, 5]
        gain_m = norm_mix[i][None, :]
        kind, j = i % N_MIXERS, i // N_MIXERS
        if kind == 0:
            h = _modulate(xs, gain_m, shift_m, scale_m)
            xs = _mla_layer(xs, h, rope, mla_w_dq[j], mla_q_norm[j], mla_w_uq[j],
                            mla_w_dkv[j], mla_kv_norm[j], mla_w_ukv[j], mla_w_o[j], gate_m)
        elif kind == 1:
            h = _modulate(xs, gain_m, shift_m, scale_m)
            xs = _sb_layer(xs, h, sb_w_qkv[j], sb_w_o[j], gate_m)
        else:
            xs = _pool_layer(xs, gain_m, shift_m, scale_m, pool_w[j].astype(BF16),
                             pool_scale[j][None, :], gate_m)
        h = _modulate(xs, norm_ffn[i][None, :], shift_f, scale_f)
        xs = _ffn_layer(xs, h, ffn_w, i, gate_f)
    return _final_norm(xs, norm_final[None, :]).reshape(b, s, d)
```

```python
import functools

import jax
import jax.numpy as jnp
from jax import lax
from jax.experimental import pallas as pl
from jax.experimental.pallas import tpu as pltpu

F32 = jnp.float32
BF16 = jnp.bfloat16

D_MODEL = 4096
SEQ = 16384
DEPTH = 4
N_MIXERS = 3
RMS_EPS = 1e-6
N_MOD = 6

MLA_HEADS = 16
MLA_Q_LORA = 1536
MLA_KV_LORA = 512
MLA_D_NOPE = 128
MLA_D_ROPE = 64
MLA_D_V = 128
ROPE_THETA = 10000.0
MLA_D_QK_PAD = 256
MLA_KVA_PAD = MLA_KV_LORA + 128

SB_HEADS = 16
SB_HEAD_DIM = 128

POOL_WINDOWS = (2, 4, 8, 16)
POOL_GROUP = D_MODEL // len(POOL_WINDOWS)
POOL_HALO = 16

D_FF = ((8 * D_MODEL + 3 * 256 - 1) // (3 * 256)) * 256
D_FF_PAD = 11264

LANES = 128
LOG2_E = 1.4426950408889634
SB_UNDERFLOW_LOG2 = 150.0
VMEM_LIMIT = 56 * 1024 * 1024


def _params(*sem):
    return pltpu.CompilerParams(dimension_semantics=sem, vmem_limit_bytes=VMEM_LIMIT)


def _ada_kernel(c_ref, w_ref, tab_ref, o_ref):
    c = c_ref[...]
    s = c * jax.nn.sigmoid(c)
    lhs = jnp.broadcast_to(s, (8, s.shape[1])).astype(BF16)
    base = jnp.dot(lhs, w_ref[...].astype(BF16), preferred_element_type=F32)
    o_ref[...] = base[0:1, :] + tab_ref[...]


def _ada_mod(c, ada_w, ada_table):
    d = c.shape[1]
    n = ada_w.shape[1]
    tn = 512
    tab = ada_table.reshape(DEPTH, n)
    return pl.pallas_call(
        _ada_kernel,
        out_shape=jax.ShapeDtypeStruct((DEPTH, n), F32),
        grid=(n // tn,),
        in_specs=[
            pl.BlockSpec((1, d), lambda j: (0, 0)),
            pl.BlockSpec((d, tn), lambda j: (0, j)),
            pl.BlockSpec((DEPTH, tn), lambda j: (0, j)),
        ],
        out_specs=pl.BlockSpec((DEPTH, tn), lambda j: (0, j)),
        compiler_params=_params("arbitrary"),
        name="ada_mod",
    )(c, ada_w, tab)


def _modulated(x, gain, shift, scale):
    ms = jnp.mean(x * x, axis=-1, keepdims=True)
    y = x * lax.rsqrt(ms + RMS_EPS) * gain
    return y * (1.0 + scale) + shift


def _modulate_kernel(x_ref, g_ref, sh_ref, sc_ref, o_ref):
    h = _modulated(x_ref[...], g_ref[...], sh_ref[...], sc_ref[...])
    o_ref[...] = h.astype(o_ref.dtype)


def _modulate(x, gain, shift, scale):
    s, d = x.shape
    tm = 512
    vec = pl.BlockSpec((1, d), lambda i: (0, 0))
    return pl.pallas_call(
        _modulate_kernel,
        out_shape=jax.ShapeDtypeStruct((s, d), BF16),
        grid=(s // tm,),
        in_specs=[pl.BlockSpec((tm, d), lambda i: (i, 0)), vec, vec, vec],
        out_specs=pl.BlockSpec((tm, d), lambda i: (i, 0)),
        compiler_params=_params("arbitrary"),
        name="modulate",
    )(x, gain, shift, scale)


def _final_norm_kernel(x_ref, g_ref, o_ref):
    x = x_ref[...]
    ms = jnp.mean(x * x, axis=-1, keepdims=True)
    o_ref[...] = x * lax.rsqrt(ms + RMS_EPS) * g_ref[...]


def _final_norm(x, gain):
    s, d = x.shape
    tm = 256
    return pl.pallas_call(
        _final_norm_kernel,
        out_shape=jax.ShapeDtypeStruct((s, d), F32),
        grid=(s // tm,),
        in_specs=[pl.BlockSpec((tm, d), lambda i: (i, 0)),
                  pl.BlockSpec((1, d), lambda i: (0, 0))],
        out_specs=pl.BlockSpec((tm, d), lambda i: (i, 0)),
        compiler_params=_params("arbitrary"),
        name="final_norm",
    )(x, gain)


def _mm_resid_kernel(a_ref, w_ref, x_ref, g_ref, o_ref, acc_ref, *, nk):
    k = pl.program_id(2)
    part = jnp.dot(a_ref[...], w_ref[...], preferred_element_type=F32)
    if nk == 1:
        o_ref[...] = x_ref[...] + g_ref[...] * part
        return

    @pl.when(k == 0)
    def _():
        acc_ref[...] = part

    @pl.when(jnp.logical_and(k > 0, k < nk - 1))
    def _():
        acc_ref[...] += part

    @pl.when(k == nk - 1)
    def _():
        o_ref[...] = x_ref[...] + g_ref[...] * (acc_ref[...] + part)


def _mm_resid(a, w, x, gate, *, tm, tn, tk, name, layer=None):
    m, kdim = a.shape
    n = w.shape[-1]
    nk = kdim // tk
    if layer is None:
        wspec = pl.BlockSpec((tk, tn), lambda i, j, k: (k, j))
    else:
        wspec = pl.BlockSpec((None, tk, tn), lambda i, j, k: (layer, k, j))
    return pl.pallas_call(
        functools.partial(_mm_resid_kernel, nk=nk),
        out_shape=jax.ShapeDtypeStruct((m, n), F32),
        grid=(m // tm, n // tn, nk),
        in_specs=[
            pl.BlockSpec((tm, tk), lambda i, j, k: (i, k)),
            wspec,
            pl.BlockSpec((tm, tn), lambda i, j, k: (i, j)),
            pl.BlockSpec((1, tn), lambda i, j, k: (0, j)),
        ],
        out_specs=pl.BlockSpec((tm, tn), lambda i, j, k: (i, j)),
        scratch_shapes=[pltpu.VMEM((tm, tn), F32)],
        compiler_params=_params("arbitrary", "arbitrary", "arbitrary"),
        name=name,
    )(a, w, x, gate)


def _mm_scale_kernel(a_ref, w_ref, s_ref, o_ref):
    acc = jnp.dot(a_ref[...], w_ref[...], preferred_element_type=F32)
    o_ref[...] = (acc * s_ref[...]).astype(o_ref.dtype)


def _mm_scale(a, w, colscale, *, tm, tn, name):
    m, kdim = a.shape
    n = w.shape[1]
    return pl.pallas_call(
        _mm_scale_kernel,
        out_shape=jax.ShapeDtypeStruct((m, n), BF16),
        grid=(m // tm, n // tn),
        in_specs=[
            pl.BlockSpec((tm, kdim), lambda i, j: (i, 0)),
            pl.BlockSpec((kdim, tn), lambda i, j: (0, j)),
            pl.BlockSpec((1, tn), lambda i, j: (0, j)),
        ],
        out_specs=pl.BlockSpec((tm, tn), lambda i, j: (i, j)),
        compiler_params=_params("arbitrary", "arbitrary"),
        name=name,
    )(a, w, colscale)


def _ffn_up_kernel(a_ref, wg_ref, wu_ref, o_ref):
    a = a_ref[...]
    g = jnp.dot(a, wg_ref[...], preferred_element_type=F32)
    u = jnp.dot(a, wu_ref[...], preferred_element_type=F32)
    o_ref[...] = (g * jax.nn.sigmoid(g) * u).astype(o_ref.dtype)


def _ffn_up(h, wg, wu, layer, *, tm, tn):
    m, kdim = h.shape
    n = wg.shape[-1]
    wspec = pl.BlockSpec((None, kdim, tn), lambda i, j: (layer, 0, j))
    return pl.pallas_call(
        _ffn_up_kernel,
        out_shape=jax.ShapeDtypeStruct((m, n), BF16),
        grid=(m // tm, n // tn),
        in_specs=[pl.BlockSpec((tm, kdim), lambda i, j: (i, 0)), wspec, wspec],
        out_specs=pl.BlockSpec((tm, tn), lambda i, j: (i, j)),
        compiler_params=_params("arbitrary", "arbitrary"),
        name="ffn_up",
    )(h, wg, wu)


def _rope_table_kernel(pos_ref, inv_ref, c_ref, s1_ref, s2_ref):
    ang = pos_ref[...] * inv_ref[...]
    cos = jnp.cos(ang)
    sin = jnp.sin(ang)
    lane = lax.broadcasted_iota(jnp.int32, ang.shape, 1)
    half = MLA_D_ROPE // 2
    c_ref[...] = jnp.where(lane < MLA_D_ROPE, cos, 0.0)
    s1_ref[...] = jnp.where(lane < half, -sin, 0.0)
    s2_ref[...] = jnp.where(jnp.logical_and(lane >= half, lane < MLA_D_ROPE), sin, 0.0)


def _rope_tables(positions):
    s = positions.shape[0]
    half = MLA_D_ROPE // 2
    inv = 1.0 / (ROPE_THETA ** (jnp.arange(0, MLA_D_ROPE, 2, dtype=F32) / MLA_D_ROPE))
    inv_lanes = jnp.concatenate([inv, inv, jnp.zeros((LANES - 2 * half,), F32)])[None, :]
    pos = positions.astype(F32)[:, None]
    tm = 1024
    out = jax.ShapeDtypeStruct((s, LANES), F32)
    tab = pl.BlockSpec((tm, LANES), lambda i: (i, 0))
    return pl.pallas_call(
        _rope_table_kernel,
        out_shape=(out, out, out),
        grid=(s // tm,),
        in_specs=[pl.BlockSpec((tm, 1), lambda i: (i, 0)),
                  pl.BlockSpec((1, LANES), lambda i: (0, 0))],
        out_specs=(tab, tab, tab),
        compiler_params=_params("arbitrary"),
        name="rope_tables",
    )(pos, inv_lanes)


def _rotate_lanes(r, c, s1, s2):
    return r * c + pltpu.roll(r, 96, axis=1) * s1 + pltpu.roll(r, 32, axis=1) * s2


def _mla_down_kernel(a_ref, wq_ref, wkv_ref, qn_ref, kvn_ref, c_ref, s1_ref, s2_ref,
                     cq_ref, ckv_ref, kr_ref, accq_ref, acckv_ref, *, nk):
    k = pl.program_id(1)
    a = a_ref[...]
    pq = jnp.dot(a, wq_ref[...], preferred_element_type=F32)
    pkv = jnp.dot(a, wkv_ref[...], preferred_element_type=F32)

    @pl.when(k == 0)
    def _():
        accq_ref[...] = pq
        acckv_ref[...] = pkv

    @pl.when(k > 0)
    def _():
        accq_ref[...] += pq
        acckv_ref[...] += pkv

    @pl.when(k == nk - 1)
    def _():
        cq = accq_ref[...]
        ms = jnp.mean(cq * cq, axis=-1, keepdims=True)
        cq_ref[...] = (cq * lax.rsqrt(ms + RMS_EPS) * qn_ref[...]).astype(cq_ref.dtype)
        ckv = acckv_ref[:, :MLA_KV_LORA]
        ms = jnp.mean(ckv * ckv, axis=-1, keepdims=True)
        ckv_ref[...] = (ckv * lax.rsqrt(ms + RMS_EPS) * kvn_ref[...]).astype(ckv_ref.dtype)
        kr = acckv_ref[:, MLA_KV_LORA:]
        kr_ref[...] = _rotate_lanes(kr, c_ref[...], s1_ref[...], s2_ref[...]).astype(kr_ref.dtype)


def _mla_down(h, w_dq, w_dkv_pad, q_norm, kv_norm, rope_c, rope_s1, rope_s2):
    m, kdim = h.shape
    tm, tk = 1024, 1024
    nk = kdim // tk
    nq = w_dq.shape[1]
    nkv = w_dkv_pad.shape[1]
    tab = pl.BlockSpec((tm, LANES), lambda i, k: (i, 0))
    return pl.pallas_call(
        functools.partial(_mla_down_kernel, nk=nk),
        out_shape=(jax.ShapeDtypeStruct((m, nq), BF16),
                   jax.ShapeDtypeStruct((m, MLA_KV_LORA), BF16),
                   jax.ShapeDtypeStruct((m, LANES), BF16)),
        grid=(m // tm, nk),
        in_specs=[
            pl.BlockSpec((tm, tk), lambda i, k: (i, k)),
            pl.BlockSpec((tk, nq), lambda i, k: (k, 0)),
            pl.BlockSpec((tk, nkv), lambda i, k: (k, 0)),
            pl.BlockSpec((1, nq), lambda i, k: (0, 0)),
            pl.BlockSpec((1, MLA_KV_LORA), lambda i, k: (0, 0)),
            tab, tab, tab,
        ],
        out_specs=(pl.BlockSpec((tm, nq), lambda i, k: (i, 0)),
                   pl.BlockSpec((tm, MLA_KV_LORA), lambda i, k: (i, 0)),
                   pl.BlockSpec((tm, LANES), lambda i, k: (i, 0))),
        scratch_shapes=[pltpu.VMEM((tm, nq), F32), pltpu.VMEM((tm, nkv), F32)],
        compiler_params=_params("arbitrary", "arbitrary"),
        name="mla_down",
    )(h, w_dq, w_dkv_pad, q_norm, kv_norm, rope_c, rope_s1, rope_s2)


def _mla_q_up_kernel(a_ref, w_ref, c_ref, s1_ref, s2_ref, o_ref, *, heads_per_tile):
    acc = jnp.dot(a_ref[...], w_ref[...], preferred_element_type=F32)
    scale = (MLA_D_NOPE + MLA_D_ROPE) ** -0.5 * LOG2_E
    c, s1, s2 = c_ref[...], s1_ref[...], s2_ref[...]
    for hh in range(heads_per_tile):
        lo = hh * MLA_D_QK_PAD
        o_ref[:, lo:lo + LANES] = (acc[:, lo:lo + LANES] * scale).astype(o_ref.dtype)
        r = _rotate_lanes(acc[:, lo + LANES:lo + 2 * LANES], c, s1, s2)
        o_ref[:, lo + LANES:lo + 2 * LANES] = (r * scale).astype(o_ref.dtype)


def _mla_q_up(cq, w_uq_pad, rope_c, rope_s1, rope_s2):
    m, kdim = cq.shape
    n = w_uq_pad.shape[1]
    tm, tn = 512, 1024
    tab = pl.BlockSpec((tm, LANES), lambda i, j: (i, 0))
    return pl.pallas_call(
        functools.partial(_mla_q_up_kernel, heads_per_tile=tn // MLA_D_QK_PAD),
        out_shape=jax.ShapeDtypeStruct((m, n), BF16),
        grid=(m // tm, n // tn),
        in_specs=[pl.BlockSpec((tm, kdim), lambda i, j: (i, 0)),
                  pl.BlockSpec((kdim, tn), lambda i, j: (0, j)),
                  tab, tab, tab],
        out_specs=pl.BlockSpec((tm, tn), lambda i, j: (i, j)),
        compiler_params=_params("arbitrary", "arbitrary"),
        name="mla_q_up",
    )(cq, w_uq_pad, rope_c, rope_s1, rope_s2)


def _mla_kv_up_kernel(a_ref, wk_ref, wv_ref, kr_ref, k_ref, v_ref):
    a = a_ref[...]
    kn = jnp.dot(a, wk_ref[...], preferred_element_type=F32)
    v_ref[...] = jnp.dot(a, wv_ref[...], preferred_element_type=F32).astype(v_ref.dtype)
    kr = kr_ref[...]
    for hh in range(MLA_HEADS):
        lo = hh * MLA_D_QK_PAD
        k_ref[:, lo:lo + LANES] = kn[:, hh * LANES:(hh + 1) * LANES].astype(k_ref.dtype)
        k_ref[:, lo + LANES:lo + 2 * LANES] = kr


def _mla_kv_up(ckv, w_uk, w_uv, k_rope):
    m, kdim = ckv.shape
    tm = 512
    nk = w_uk.shape[1]
    nv = w_uv.shape[1]
    return pl.pallas_call(
        _mla_kv_up_kernel,
        out_shape=(jax.ShapeDtypeStruct((m, MLA_HEADS * MLA_D_QK_PAD), BF16),
                   jax.ShapeDtypeStruct((m, nv), BF16)),
        grid=(m // tm,),
        in_specs=[pl.BlockSpec((tm, kdim), lambda i: (i, 0)),
                  pl.BlockSpec((kdim, nk), lambda i: (0, 0)),
                  pl.BlockSpec((kdim, nv), lambda i: (0, 0)),
                  pl.BlockSpec((tm, LANES), lambda i: (i, 0))],
        out_specs=(pl.BlockSpec((tm, MLA_HEADS * MLA_D_QK_PAD), lambda i: (i, 0)),
                   pl.BlockSpec((tm, nv), lambda i: (i, 0))),
        compiler_params=_params("arbitrary"),
        name="mla_kv_up",
    )(ckv, w_uk, w_uv, k_rope)


def _qk(q, k):
    return lax.dot_general(q, k, (((1,), (1,)), ((), ())), preferred_element_type=F32)


def _mla_attn_kernel(q_ref, k_ref, v_ref, o_ref, *scratch, tb, nsub):
    m_ref, l_ref, acc_ref = scratch[:nsub], scratch[nsub:2 * nsub], scratch[2 * nsub:]
    i = pl.program_id(1)
    first = i * nsub
    lower = (lax.broadcasted_iota(jnp.int32, (tb, tb), 1)
             <= lax.broadcasted_iota(jnp.int32, (tb, tb), 0))

    def steps(chains, masked):
        ss, m_new = {}, {}
        for r, j in chains:
            start = pl.multiple_of(j * tb, tb)
            s = _qk(q_ref[r * tb:(r + 1) * tb, :], k_ref[pl.ds(start, tb), :])
            ss[r] = jnp.where(lower, s, -jnp.inf) if masked else s
        for r, _ in chains:
            m_new[r] = jnp.maximum(m_ref[r][...], jnp.max(ss[r], axis=-1, keepdims=True))
        for r, j in chains:
            start = pl.multiple_of(j * tb, tb)
            alpha = jnp.exp2(m_ref[r][...] - m_new[r])
            p = jnp.exp2(ss[r] - jnp.concatenate([m_new[r]] * (tb // LANES), axis=1))
            l_ref[r][...] = alpha * l_ref[r][...] + jnp.sum(p, axis=-1, keepdims=True)
            m_ref[r][...] = m_new[r]
            pv = jnp.dot(p.astype(BF16), v_ref[pl.ds(start, tb), :],
                         preferred_element_type=F32)
            acc_ref[r][...] = alpha * acc_ref[r][...] + pv

    for r in range(nsub):
        m_ref[r][...] = jnp.full((tb, LANES), -jnp.inf, F32)
        l_ref[r][...] = jnp.zeros((tb, LANES), F32)
        acc_ref[r][...] = jnp.zeros((tb, MLA_D_V), F32)
    for d in range(nsub):
        steps([(r, first + r - d) for r in range(d, nsub)], d == 0)

    def body(j, carry):
        steps([(r, j) for r in range(nsub)], False)
        return carry

    lax.fori_loop(0, first, body, 0)
    for r in range(nsub):
        o_ref[r * tb:(r + 1) * tb, :] = (acc_ref[r][...] / l_ref[r][...]).astype(o_ref.dtype)


def _mla_attention(q, k, v, *, tb, nsub):
    s = q.shape[0]
    tq = tb * nsub
    return pl.pallas_call(
        functools.partial(_mla_attn_kernel, tb=tb, nsub=nsub),
        out_shape=jax.ShapeDtypeStruct((s, MLA_HEADS * MLA_D_V), BF16),
        grid=(MLA_HEADS, s // tq),
        in_specs=[pl.BlockSpec((tq, MLA_D_QK_PAD), lambda h, i: (i, h)),
                  pl.BlockSpec((s, MLA_D_QK_PAD), lambda h, i: (0, h)),
                  pl.BlockSpec((s, MLA_D_V), lambda h, i: (0, h))],
        out_specs=pl.BlockSpec((tq, MLA_D_V), lambda h, i: (i, h)),
        scratch_shapes=([pltpu.VMEM((tb, LANES), F32)] * (2 * nsub)
                        + [pltpu.VMEM((tb, MLA_D_V), F32)] * nsub),
        compiler_params=_params("arbitrary", "arbitrary"),
        name="mla_attention",
    )(q, k, v)


def _sb_attn_kernel(q_ref, k_ref, v_ref, o_ref, carry_ref, acc_ref, knpm_ref, *, tb, nsub):
    i = pl.program_id(1)
    first = i * nsub
    strict = (lax.broadcasted_iota(jnp.int32, (tb, tb), 1)
              < lax.broadcasted_iota(jnp.int32, (tb, tb), 0))
    tri2 = ((lax.broadcasted_iota(jnp.int32, (2 * tb, tb), 0) & (tb - 1))
            >= lax.broadcasted_iota(jnp.int32, (2 * tb, tb), 1)).astype(BF16)

    def steps(chains, masked):
        zs, hls, cums, as_ = {}, {}, {}, {}

        def qk(r, j):
            zs[r] = _qk(q_ref[r * tb:(r + 1) * tb, :],
                        k_ref[pl.ds(pl.multiple_of(j * tb, tb), tb), :])

        def softplus(r, j):
            z = zs[r]
            neg_abs = lax.bitcast_convert_type(
                lax.bitcast_convert_type(z, jnp.uint32) | jnp.uint32(0x80000000), F32)
            sp = jnp.maximum(z, 0.0) + jnp.log2(1.0 + jnp.exp2(neg_abs))
            sp = jnp.where(strict, sp, 0.0) if masked else sp
            hi = sp.astype(BF16)
            lo = (sp - hi.astype(F32)).astype(BF16)
            hls[r] = jnp.concatenate([hi, lo], axis=1)

        def cumsum(r, j):
            cums[r] = jnp.dot(hls[r], tri2, preferred_element_type=F32)

        def weights(r, j):
            a = jnp.exp2(zs[r] - (cums[r] + carry_ref[r]))
            as_[r] = (jnp.where(strict, a, 0.0) if masked else a).astype(BF16)
            carry_ref[r] += cums[r][:, 0:1]

        def av(r, j):
            acc_ref[r] += jnp.dot(as_[r], v_ref[pl.ds(pl.multiple_of(j * tb, tb), tb), :],
                                  preferred_element_type=F32)

        for stage in (qk, softplus, cumsum, weights, av):
            for r, j in chains:
                stage(r, j)

    @pl.when(i == 0)
    def _():
        def tile_norm(j, running):
            kf = k_ref[pl.ds(pl.multiple_of(j * tb, tb), tb), :].astype(F32)
            kn2 = jnp.max(jnp.sum(kf * kf, axis=-1, keepdims=True), axis=0, keepdims=True)
            running = jnp.maximum(running, kn2)
            knpm_ref[pl.ds(j, 1), :] = jnp.broadcast_to(running, (1, LANES))
            return running

        lax.fori_loop(0, k_ref.shape[0] // tb, tile_norm, jnp.zeros((1, 1), F32))

    qf = q_ref[...].astype(F32)
    qn2 = jnp.max(jnp.sum(qf * qf, axis=-1, keepdims=True), axis=0, keepdims=True)

    def may_contribute(j):
        cmin = carry_ref[0]
        for r in range(1, nsub):
            cmin = jnp.minimum(cmin, carry_ref[r])
        slack = jnp.min(cmin, axis=0, keepdims=True) - SB_UNDERFLOW_LOG2
        kn2 = knpm_ref[pl.ds(jnp.maximum(j, 0), 1), :]
        live = jnp.logical_or(slack <= 0.0, qn2 * kn2 * 1.01 >= slack * slack)
        return (jnp.max(jnp.where(live, 1.0, 0.0)) > 0.5).astype(jnp.int32)

    carry_ref[...] = jnp.zeros(carry_ref.shape, F32)
    acc_ref[...] = jnp.zeros(acc_ref.shape, F32)
    for d in range(nsub):
        steps([(r, first + r - d) for r in range(d, nsub)], d == 0)

    def cond(state):
        jj, live = state
        return jnp.logical_and(jj < first, live > 0)

    def body(state):
        jj, _ = state
        j = first - 1 - jj
        steps([(r, j) for r in range(nsub)], False)
        return jj + 1, may_contribute(j - 1)

    lax.while_loop(cond, body, (jnp.int32(0), may_contribute(first - 1)))
    o_ref[...] = acc_ref[...].reshape(o_ref.shape).astype(o_ref.dtype)


def _sb_attention(qkv, *, tb, nsub):
    s = qkv.shape[0]
    hd = SB_HEAD_DIM
    tq = tb * nsub
    return pl.pallas_call(
        functools.partial(_sb_attn_kernel, tb=tb, nsub=nsub),
        out_shape=jax.ShapeDtypeStruct((s, SB_HEADS * hd), BF16),
        grid=(SB_HEADS, s // tq),
        in_specs=[pl.BlockSpec((tq, hd), lambda h, i: (i, h)),
                  pl.BlockSpec((s, hd), lambda h, i: (0, SB_HEADS + h)),
                  pl.BlockSpec((s, hd), lambda h, i: (0, 2 * SB_HEADS + h))],
        out_specs=pl.BlockSpec((tq, hd), lambda h, i: (i, h)),
        scratch_shapes=[pltpu.VMEM((nsub, tb, 1), F32), pltpu.VMEM((nsub, tb, hd), F32),
                        pltpu.VMEM((s // tb, LANES), F32)],
        compiler_params=_params("arbitrary", "arbitrary"),
        name="sb_attention",
    )(qkv, qkv, qkv)


def _pool_kernel(x_ref, halo_ref, g_ref, sh_ref, sc_ref, w_ref, ps_ref, gate_ref,
                 o_ref, hbuf_ref, *, tm):
    i = pl.program_id(0)
    g, sh, sc = g_ref[...], sh_ref[...], sc_ref[...]
    x = x_ref[...]
    h = _modulated(x, g, sh, sc)
    h_halo = _modulated(halo_ref[...], g, sh, sc)
    hbuf_ref[:POOL_HALO, :] = jnp.where(i > 0, h_halo, 0.0)
    hbuf_ref[POOL_HALO:, :] = h
    t = i * tm + lax.broadcasted_iota(jnp.int32, (tm, 1), 0)
    for gi, w in enumerate(POOL_WINDOWS):
        lo, hi = gi * POOL_GROUP, (gi + 1) * POOL_GROUP
        ssum = h[:, lo:hi]
        for k in range(1, w):
            ssum = ssum + hbuf_ref[POOL_HALO - k:POOL_HALO - k + tm, lo:hi]
        cnt = jnp.minimum(t + 1, w).astype(F32)
        p = ssum / cnt - h[:, lo:hi]
        y = jnp.dot(p.astype(BF16), w_ref[gi], preferred_element_type=F32)
        o_ref[:, lo:hi] = x[:, lo:hi] + gate_ref[:, lo:hi] * (y * ps_ref[:, lo:hi])


def _pool_layer(x, gain, shift, scale, w_pool, pool_scale, gate):
    s, d = x.shape
    tm = 256
    ng = len(POOL_WINDOWS)
    vec = pl.BlockSpec((1, d), lambda i: (0, 0))
    halo_blocks = tm // POOL_HALO
    return pl.pallas_call(
        functools.partial(_pool_kernel, tm=tm),
        out_shape=jax.ShapeDtypeStruct((s, d), F32),
        grid=(s // tm,),
        in_specs=[pl.BlockSpec((tm, d), lambda i: (i, 0)),
                  pl.BlockSpec((POOL_HALO, d),
                               lambda i: (jnp.maximum(i * halo_blocks - 1, 0), 0)),
                  vec, vec, vec,
                  pl.BlockSpec((ng, POOL_GROUP, POOL_GROUP), lambda i: (0, 0, 0)),
                  vec, vec],
        out_specs=pl.BlockSpec((tm, d), lambda i: (i, 0)),
        scratch_shapes=[pltpu.VMEM((tm + POOL_HALO, d), F32)],
        compiler_params=_params("arbitrary"),
        name="pool_layer",
    )(x, x, gain, shift, scale, w_pool, pool_scale, gate)


def _pad_cols(w, n):
    return jnp.pad(w, ((0, 0), (0, n - w.shape[1])))


def _mla_weights(w_dq, w_uq, w_dkv, w_ukv, w_o):
    h = MLA_HEADS
    w_uq_h = w_uq.reshape(MLA_Q_LORA, h, MLA_D_NOPE + MLA_D_ROPE)
    w_uq_pad = jnp.pad(w_uq_h, ((0, 0), (0, 0), (0, MLA_D_QK_PAD - MLA_D_NOPE - MLA_D_ROPE)))
    w_uq_pad = w_uq_pad.reshape(MLA_Q_LORA, h * MLA_D_QK_PAD).astype(BF16)
    w_ukv_h = w_ukv.reshape(MLA_KV_LORA, h, MLA_D_NOPE + MLA_D_V)
    w_uk = w_ukv_h[:, :, :MLA_D_NOPE].reshape(MLA_KV_LORA, h * MLA_D_NOPE).astype(BF16)
    w_uv = w_ukv_h[:, :, MLA_D_NOPE:].reshape(MLA_KV_LORA, h * MLA_D_V).astype(BF16)
    w_dkv_pad = _pad_cols(w_dkv, MLA_KVA_PAD).astype(BF16)
    return w_dq.astype(BF16), w_uq_pad, w_dkv_pad, w_uk, w_uv, w_o.astype(BF16)


def _mla_layer(x, h, rope, w_dq, q_norm, w_uq, w_dkv, kv_norm, w_ukv, w_o, gate):
    w_dq_b, w_uq_pad, w_dkv_pad, w_uk, w_uv, w_o_b = _mla_weights(w_dq, w_uq, w_dkv, w_ukv, w_o)
    cq, ckv, k_rope = _mla_down(h, w_dq_b, w_dkv_pad, q_norm[None, :], kv_norm[None, :], *rope)
    q = _mla_q_up(cq, w_uq_pad, *rope)
    k, v = _mla_kv_up(ckv, w_uk, w_uv, k_rope)
    o = _mla_attention(q, k, v, tb=1024, nsub=2)
    return _mm_resid(o, w_o_b, x, gate, tm=1024, tn=1024, tk=o.shape[1], name="mla_out")


def _sb_layer(x, h, w_qkv, w_o, gate):
    n = w_qkv.shape[1]
    colscale = jnp.where(jnp.arange(n) < SB_HEADS * SB_HEAD_DIM,
                         SB_HEAD_DIM ** -0.5 * LOG2_E, 1.0).astype(F32)[None, :]
    qkv = _mm_scale(h, w_qkv.astype(BF16), colscale, tm=1024, tn=1024, name="sb_qkv")
    o = _sb_attention(qkv, tb=256, nsub=4)
    return _mm_resid(o, w_o.astype(BF16), x, gate, tm=1024, tn=1024, tk=o.shape[1],
                     name="sb_out")


def _cast_pad_cols_kernel(w_ref, o_ref):
    n = w_ref.shape[1]
    o_ref[:, :n] = w_ref[...].astype(o_ref.dtype)
    o_ref[:, n:] = jnp.zeros((o_ref.shape[0], o_ref.shape[1] - n), o_ref.dtype)


def _cast_pad_cols(w, n_pad):
    nl, k, n = w.shape
    tr = 256
    return pl.pallas_call(
        _cast_pad_cols_kernel,
        out_shape=jax.ShapeDtypeStruct((nl, k, n_pad), BF16),
        grid=(nl, k // tr),
        in_specs=[pl.BlockSpec((None, tr, n), lambda l, i: (l, i, 0))],
        out_specs=pl.BlockSpec((None, tr, n_pad), lambda l, i: (l, i, 0)),
        compiler_params=_params("arbitrary", "arbitrary"),
        name="cast_pad_cols",
    )(w)


def _cast_pad_rows_kernel(w_ref, o_ref, *, n_in_blocks):
    i = pl.program_id(1)

    @pl.when(i < n_in_blocks)
    def _():
        o_ref[...] = w_ref[...].astype(o_ref.dtype)

    @pl.when(i >= n_in_blocks)
    def _():
        o_ref[...] = jnp.zeros(o_ref.shape, o_ref.dtype)


def _cast_pad_rows(w, k_pad):
    nl, k, n = w.shape
    tr = 256
    n_in = k // tr
    return pl.pallas_call(
        functools.partial(_cast_pad_rows_kernel, n_in_blocks=n_in),
        out_shape=jax.ShapeDtypeStruct((nl, k_pad, n), BF16),
        grid=(nl, k_pad // tr),
        in_specs=[pl.BlockSpec((None, tr, n), lambda l, i: (l, jnp.minimum(i, n_in - 1), 0))],
        out_specs=pl.BlockSpec((None, tr, n), lambda l, i: (l, i, 0)),
        compiler_params=_params("arbitrary", "arbitrary"),
        name="cast_pad_rows",
    )(w)


def _ffn_weights(w_gate, w_up, w_down):
    return (_cast_pad_cols(w_gate, D_FF_PAD), _cast_pad_cols(w_up, D_FF_PAD),
            _cast_pad_rows(w_down, D_FF_PAD))


def _ffn_layer(x, h, ffn_w, layer, gate):
    wg, wu, wd = ffn_w
    act = _ffn_up(h, wg, wu, layer, tm=1024, tn=512)
    return _mm_resid(act, wd, x, gate, tm=512, tn=512, tk=D_FF_PAD, name="ffn_down",
                     layer=layer)


def kernel(x, c, positions, ada_w, ada_table, norm_mix, norm_ffn, norm_final,
           mla_w_dq, mla_q_norm, mla_w_uq, mla_w_dkv, mla_kv_norm, mla_w_ukv, mla_w_o,
           sb_w_qkv, sb_w_o, pool_w, pool_scale, ffn_w_gate, ffn_w_up, ffn_w_down):
    b, s, d = x.shape
    assert b == 1 and c.shape == (1, d)
    xs = x.reshape(s, d)
    mod = _ada_mod(c, ada_w, ada_table).reshape(DEPTH, N_MOD, 1, d)
    rope = _rope_tables(positions.reshape(s))
    ffn_w = _ffn_weights(ffn_w_gate, ffn_w_up, ffn_w_down)
    for i in range(DEPTH):
        shift_m, scale_m, gate_m = mod[i, 0], mod[i, 1], mod[i, 2]
        shift_f, scale_f, gate_f = mod[i, 3], mod[i, 4], mod[i, 5]
        gain_m = norm_mix[i][None, :]
        kind, j = i % N_MIXERS, i // N_MIXERS
        if kind == 0:
            h = _modulate(xs, gain_m, shift_m, scale_m)
            xs = _mla_layer(xs, h, rope, mla_w_dq[j], mla_q_norm[j], mla_w_uq[j],
                            mla_w_dkv[j], mla_kv_norm[j], mla_w_ukv[j], mla_w_o[j], gate_m)
        elif kind == 1:
            h = _modulate(xs, gain_m, shift_m, scale_m)
            xs = _sb_layer(xs, h, sb_w_qkv[j], sb_w_o[j], gate_m)
        else:
            xs = _pool_layer(xs, gain_m, shift_m, scale_m, pool_w[j].astype(BF16),
                             pool_scale[j][None, :], gate_m)
        h = _modulate(xs, norm_ffn[i][None, :], shift_f, scale_f)
        xs = _ffn_layer(xs, h, ffn_w, i, gate_f)
    return _final_norm(xs, norm_final[None, :]).reshape(b, s, d)
```

```python
import functools

import jax
import jax.numpy as jnp
from jax import lax
from jax.experimental import pallas as pl
from jax.experimental.pallas import tpu as pltpu

F32 = jnp.float32
BF16 = jnp.bfloat16

D_MODEL = 4096
SEQ = 16384
DEPTH = 4
N_MIXERS = 3
RMS_EPS = 1e-6
N_MOD = 6

MLA_HEADS = 16
MLA_Q_LORA = 1536
MLA_KV_LORA = 512
MLA_D_NOPE = 128
MLA_D_ROPE = 64
MLA_D_V = 128
ROPE_THETA = 10000.0
MLA_D_QK_PAD = 256
MLA_KVA_PAD = MLA_KV_LORA + 128

SB_HEADS = 16
SB_HEAD_DIM = 128

POOL_WINDOWS = (2, 4, 8, 16)
POOL_GROUP = D_MODEL // len(POOL_WINDOWS)
POOL_HALO = 16

D_FF = ((8 * D_MODEL + 3 * 256 - 1) // (3 * 256)) * 256
D_FF_PAD = 11264

LANES = 128
LOG2_E = 1.4426950408889634
SB_UNDERFLOW_LOG2 = 150.0
VMEM_LIMIT = 56 * 1024 * 1024


def _params(*sem):
    return pltpu.CompilerParams(dimension_semantics=sem, vmem_limit_bytes=VMEM_LIMIT)


def _ada_kernel(c_ref, w_ref, tab_ref, o_ref):
    c = c_ref[...]
    s = c * jax.nn.sigmoid(c)
    lhs = jnp.broadcast_to(s, (8, s.shape[1])).astype(BF16)
    base = jnp.dot(lhs, w_ref[...].astype(BF16), preferred_element_type=F32)
    o_ref[...] = base[0:1, :] + tab_ref[...]


def _ada_mod(c, ada_w, ada_table):
    d = c.shape[1]
    n = ada_w.shape[1]
    tn = 512
    tab = ada_table.reshape(DEPTH, n)
    return pl.pallas_call(
        _ada_kernel,
        out_shape=jax.ShapeDtypeStruct((DEPTH, n), F32),
        grid=(n // tn,),
        in_specs=[
            pl.BlockSpec((1, d), lambda j: (0, 0)),
            pl.BlockSpec((d, tn), lambda j: (0, j)),
            pl.BlockSpec((DEPTH, tn), lambda j: (0, j)),
        ],
        out_specs=pl.BlockSpec((DEPTH, tn), lambda j: (0, j)),
        compiler_params=_params("arbitrary"),
        name="ada_mod",
    )(c, ada_w, tab)


def _modulated(x, gain, shift, scale):
    ms = jnp.mean(x * x, axis=-1, keepdims=True)
    y = x * lax.rsqrt(ms + RMS_EPS) * gain
    return y * (1.0 + scale) + shift


def _modulate_kernel(x_ref, g_ref, sh_ref, sc_ref, o_ref):
    h = _modulated(x_ref[...], g_ref[...], sh_ref[...], sc_ref[...])
    o_ref[...] = h.astype(o_ref.dtype)


def _modulate(x, gain, shift, scale):
    s, d = x.shape
    tm = 512
    vec = pl.BlockSpec((1, d), lambda i: (0, 0))
    return pl.pallas_call(
        _modulate_kernel,
        out_shape=jax.ShapeDtypeStruct((s, d), BF16),
        grid=(s // tm,),
        in_specs=[pl.BlockSpec((tm, d), lambda i: (i, 0)), vec, vec, vec],
        out_specs=pl.BlockSpec((tm, d), lambda i: (i, 0)),
        compiler_params=_params("arbitrary"),
        name="modulate",
    )(x, gain, shift, scale)


def _final_norm_kernel(x_ref, g_ref, o_ref):
    x = x_ref[...]
    ms = jnp.mean(x * x, axis=-1, keepdims=True)
    o_ref[...] = x * lax.rsqrt(ms + RMS_EPS) * g_ref[...]


def _final_norm(x, gain):
    s, d = x.shape
    tm = 256
    return pl.pallas_call(
        _final_norm_kernel,
        out_shape=jax.ShapeDtypeStruct((s, d), F32),
        grid=(s // tm,),
        in_specs=[pl.BlockSpec((tm, d), lambda i: (i, 0)),
                  pl.BlockSpec((1, d), lambda i: (0, 0))],
        out_specs=pl.BlockSpec((tm, d), lambda i: (i, 0)),
        compiler_params=_params("arbitrary"),
        name="final_norm",
    )(x, gain)


def _mm_resid_kernel(a_ref, w_ref, x_ref, g_ref, o_ref, acc_ref, *, nk):
    k = pl.program_id(2)
    part = jnp.dot(a_ref[...], w_ref[...], preferred_element_type=F32)
    if nk == 1:
        o_ref[...] = x_ref[...] + g_ref[...] * part
        return

    @pl.when(k == 0)
    def _():
        acc_ref[...] = part

    @pl.when(jnp.logical_and(k > 0, k < nk - 1))
    def _():
        acc_ref[...] += part

    @pl.when(k == nk - 1)
    def _():
        o_ref[...] = x_ref[...] + g_ref[...] * (acc_ref[...] + part)


def _mm_resid(a, w, x, gate, *, tm, tn, tk, name, layer=None):
    m, kdim = a.shape
    n = w.shape[-1]
    nk = kdim // tk
    if layer is None:
        wspec = pl.BlockSpec((tk, tn), lambda i, j, k: (k, j))
    else:
        wspec = pl.BlockSpec((None, tk, tn), lambda i, j, k: (layer, k, j))
    return pl.pallas_call(
        functools.partial(_mm_resid_kernel, nk=nk),
        out_shape=jax.ShapeDtypeStruct((m, n), F32),
        grid=(m // tm, n // tn, nk),
        in_specs=[
            pl.BlockSpec((tm, tk), lambda i, j, k: (i, k)),
            wspec,
            pl.BlockSpec((tm, tn), lambda i, j, k: (i, j)),
            pl.BlockSpec((1, tn), lambda i, j, k: (0, j)),
        ],
        out_specs=pl.BlockSpec((tm, tn), lambda i, j, k: (i, j)),
        scratch_shapes=[pltpu.VMEM((tm, tn), F32)],
        compiler_params=_params("arbitrary", "arbitrary", "arbitrary"),
        name=name,
    )(a, w, x, gate)


def _mm_scale_kernel(a_ref, w_ref, s_ref, o_ref):
    acc = jnp.dot(a_ref[...], w_ref[...], preferred_element_type=F32)
    o_ref[...] = (acc * s_ref[...]).astype(o_ref.dtype)


def _mm_scale(a, w, colscale, *, tm, tn, name):
    m, kdim = a.shape
    n = w.shape[1]
    return pl.pallas_call(
        _mm_scale_kernel,
        out_shape=jax.ShapeDtypeStruct((m, n), BF16),
        grid=(m // tm, n // tn),
        in_specs=[
            pl.BlockSpec((tm, kdim), lambda i, j: (i, 0)),
            pl.BlockSpec((kdim, tn), lambda i, j: (0, j)),
            pl.BlockSpec((1, tn), lambda i, j: (0, j)),
        ],
        out_specs=pl.BlockSpec((tm, tn), lambda i, j: (i, j)),
        compiler_params=_params("arbitrary", "arbitrary"),
        name=name,
    )(a, w, colscale)


def _ffn_up_kernel(a_ref, wg_ref, wu_ref, o_ref):
    a = a_ref[...]
    g = jnp.dot(a, wg_ref[...], preferred_element_type=F32)
    u = jnp.dot(a, wu_ref[...], preferred_element_type=F32)
    o_ref[...] = (g * jax.nn.sigmoid(g) * u).astype(o_ref.dtype)


def _ffn_up(h, wg, wu, layer, *, tm, tn):
    m, kdim = h.shape
    n = wg.shape[-1]
    wspec = pl.BlockSpec((None, kdim, tn), lambda i, j: (layer, 0, j))
    return pl.pallas_call(
        _ffn_up_kernel,
        out_shape=jax.ShapeDtypeStruct((m, n), BF16),
        grid=(m // tm, n // tn),
        in_specs=[pl.BlockSpec((tm, kdim), lambda i, j: (i, 0)), wspec, wspec],
        out_specs=pl.BlockSpec((tm, tn), lambda i, j: (i, j)),
        compiler_params=_params("arbitrary", "arbitrary"),
        name="ffn_up",
    )(h, wg, wu)


def _rope_table_kernel(pos_ref, inv_ref, c_ref, s1_ref, s2_ref):
    ang = pos_ref[...] * inv_ref[...]
    cos = jnp.cos(ang)
    sin = jnp.sin(ang)
    lane = lax.broadcasted_iota(jnp.int32, ang.shape, 1)
    half = MLA_D_ROPE // 2
    c_ref[...] = jnp.where(lane < MLA_D_ROPE, cos, 0.0)
    s1_ref[...] = jnp.where(lane < half, -sin, 0.0)
    s2_ref[...] = jnp.where(jnp.logical_and(lane >= half, lane < MLA_D_ROPE), sin, 0.0)


def _rope_tables(positions):
    s = positions.shape[0]
    half = MLA_D_ROPE // 2
    inv = 1.0 / (ROPE_THETA ** (jnp.arange(0, MLA_D_ROPE, 2, dtype=F32) / MLA_D_ROPE))
    inv_lanes = jnp.concatenate([inv, inv, jnp.zeros((LANES - 2 * half,), F32)])[None, :]
    pos = positions.astype(F32)[:, None]
    tm = 1024
    out = jax.ShapeDtypeStruct((s, LANES), F32)
    tab = pl.BlockSpec((tm, LANES), lambda i: (i, 0))
    return pl.pallas_call(
        _rope_table_kernel,
        out_shape=(out, out, out),
        grid=(s // tm,),
        in_specs=[pl.BlockSpec((tm, 1), lambda i: (i, 0)),
                  pl.BlockSpec((1, LANES), lambda i: (0, 0))],
        out_specs=(tab, tab, tab),
        compiler_params=_params("arbitrary"),
        name="rope_tables",
    )(pos, inv_lanes)


def _rotate_lanes(r, c, s1, s2):
    return r * c + pltpu.roll(r, 96, axis=1) * s1 + pltpu.roll(r, 32, axis=1) * s2


def _mla_down_kernel(a_ref, wq_ref, wkv_ref, qn_ref, kvn_ref, c_ref, s1_ref, s2_ref,
                     cq_ref, ckv_ref, kr_ref, accq_ref, acckv_ref, *, nk):
    k = pl.program_id(1)
    a = a_ref[...]
    pq = jnp.dot(a, wq_ref[...], preferred_element_type=F32)
    pkv = jnp.dot(a, wkv_ref[...], preferred_element_type=F32)

    @pl.when(k == 0)
    def _():
        accq_ref[...] = pq
        acckv_ref[...] = pkv

    @pl.when(k > 0)
    def _():
        accq_ref[...] += pq
        acckv_ref[...] += pkv

    @pl.when(k == nk - 1)
    def _():
        cq = accq_ref[...]
        ms = jnp.mean(cq * cq, axis=-1, keepdims=True)
        cq_ref[...] = (cq * lax.rsqrt(ms + RMS_EPS) * qn_ref[...]).astype(cq_ref.dtype)
        ckv = acckv_ref[:, :MLA_KV_LORA]
        ms = jnp.mean(ckv * ckv, axis=-1, keepdims=True)
        ckv_ref[...] = (ckv * lax.rsqrt(ms + RMS_EPS) * kvn_ref[...]).astype(ckv_ref.dtype)
        kr = acckv_ref[:, MLA_KV_LORA:]
        kr_ref[...] = _rotate_lanes(kr, c_ref[...], s1_ref[...], s2_ref[...]).astype(kr_ref.dtype)


def _mla_down(h, w_dq, w_dkv_pad, q_norm, kv_norm, rope_c, rope_s1, rope_s2):
    m, kdim = h.shape
    tm, tk = 1024, 1024
    nk = kdim // tk
    nq = w_dq.shape[1]
    nkv = w_dkv_pad.shape[1]
    tab = pl.BlockSpec((tm, LANES), lambda i, k: (i, 0))
    return pl.pallas_call(
        functools.partial(_mla_down_kernel, nk=nk),
        out_shape=(jax.ShapeDtypeStruct((m, nq), BF16),
                   jax.ShapeDtypeStruct((m, MLA_KV_LORA), BF16),
                   jax.ShapeDtypeStruct((m, LANES), BF16)),
        grid=(m // tm, nk),
        in_specs=[
            pl.BlockSpec((tm, tk), lambda i, k: (i, k)),
            pl.BlockSpec((tk, nq), lambda i, k: (k, 0)),
            pl.BlockSpec((tk, nkv), lambda i, k: (k, 0)),
            pl.BlockSpec((1, nq), lambda i, k: (0, 0)),
            pl.BlockSpec((1, MLA_KV_LORA), lambda i, k: (0, 0)),
            tab, tab, tab,
        ],
        out_specs=(pl.BlockSpec((tm, nq), lambda i, k: (i, 0)),
                   pl.BlockSpec((tm, MLA_KV_LORA), lambda i, k: (i, 0)),
                   pl.BlockSpec((tm, LANES), lambda i, k: (i, 0))),
        scratch_shapes=[pltpu.VMEM((tm, nq), F32), pltpu.VMEM((tm, nkv), F32)],
        compiler_params=_params("arbitrary", "arbitrary"),
        name="mla_down",
    )(h, w_dq, w_dkv_pad, q_norm, kv_norm, rope_c, rope_s1, rope_s2)


def _mla_q_up_kernel(a_ref, w_ref, c_ref, s1_ref, s2_ref, o_ref, *, heads_per_tile):
    acc = jnp.dot(a_ref[...], w_ref[...], preferred_element_type=F32)
    scale = (MLA_D_NOPE + MLA_D_ROPE) ** -0.5 * LOG2_E
    c, s1, s2 = c_ref[...], s1_ref[...], s2_ref[...]
    for hh in range(heads_per_tile):
        lo = hh * MLA_D_QK_PAD
        o_ref[:, lo:lo + LANES] = (acc[:, lo:lo + LANES] * scale).astype(o_ref.dtype)
        r = _rotate_lanes(acc[:, lo + LANES:lo + 2 * LANES], c, s1, s2)
        o_ref[:, lo + LANES:lo + 2 * LANES] = (r * scale).astype(o_ref.dtype)


def _mla_q_up(cq, w_uq_pad, rope_c, rope_s1, rope_s2):
    m, kdim = cq.shape
    n = w_uq_pad.shape[1]
    tm, tn = 512, 1024
    tab = pl.BlockSpec((tm, LANES), lambda i, j: (i, 0))
    return pl.pallas_call(
        functools.partial(_mla_q_up_kernel, heads_per_tile=tn // MLA_D_QK_PAD),
        out_shape=jax.ShapeDtypeStruct((m, n), BF16),
        grid=(m // tm, n // tn),
        in_specs=[pl.BlockSpec((tm, kdim), lambda i, j: (i, 0)),
                  pl.BlockSpec((kdim, tn), lambda i, j: (0, j)),
                  tab, tab, tab],
        out_specs=pl.BlockSpec((tm, tn), lambda i, j: (i, j)),
        compiler_params=_params("arbitrary", "arbitrary"),
        name="mla_q_up",
    )(cq, w_uq_pad, rope_c, rope_s1, rope_s2)


def _mla_kv_up_kernel(a_ref, wk_ref, wv_ref, kr_ref, k_ref, v_ref):
    a = a_ref[...]
    kn = jnp.dot(a, wk_ref[...], preferred_element_type=F32)
    v_ref[...] = jnp.dot(a, wv_ref[...], preferred_element_type=F32).astype(v_ref.dtype)
    kr = kr_ref[...]
    for hh in range(MLA_HEADS):
        lo = hh * MLA_D_QK_PAD
        k_ref[:, lo:lo + LANES] = kn[:, hh * LANES:(hh + 1) * LANES].astype(k_ref.dtype)
        k_ref[:, lo + LANES:lo + 2 * LANES] = kr


def _mla_kv_up(ckv, w_uk, w_uv, k_rope):
    m, kdim = ckv.shape
    tm = 512
    nk = w_uk.shape[1]
    nv = w_uv.shape[1]
    return pl.pallas_call(
        _mla_kv_up_kernel,
        out_shape=(jax.ShapeDtypeStruct((m, MLA_HEADS * MLA_D_QK_PAD), BF16),
                   jax.ShapeDtypeStruct((m, nv), BF16)),
        grid=(m // tm,),
        in_specs=[pl.BlockSpec((tm, kdim), lambda i: (i, 0)),
                  pl.BlockSpec((kdim, nk), lambda i: (0, 0)),
                  pl.BlockSpec((kdim, nv), lambda i: (0, 0)),
                  pl.BlockSpec((tm, LANES), lambda i: (i, 0))],
        out_specs=(pl.BlockSpec((tm, MLA_HEADS * MLA_D_QK_PAD), lambda i: (i, 0)),
                   pl.BlockSpec((tm, nv), lambda i: (i, 0))),
        compiler_params=_params("arbitrary"),
        name="mla_kv_up",
    )(ckv, w_uk, w_uv, k_rope)


def _qk(q, k):
    return lax.dot_general(q, k, (((1,), (1,)), ((), ())), preferred_element_type=F32)


def _mla_attn_kernel(q_ref, k_ref, v_ref, o_ref, *scratch, tb, nsub):
    m_ref, l_ref, acc_ref = scratch[:nsub], scratch[nsub:2 * nsub], scratch[2 * nsub:]
    i = pl.program_id(1)
    first = i * nsub
    lower = (lax.broadcasted_iota(jnp.int32, (tb, tb), 1)
             <= lax.broadcasted_iota(jnp.int32, (tb, tb), 0))

    def steps(chains, masked):
        ss, m_new = {}, {}
        for r, j in chains:
            start = pl.multiple_of(j * tb, tb)
            s = _qk(q_ref[r * tb:(r + 1) * tb, :], k_ref[pl.ds(start, tb), :])
            ss[r] = jnp.where(lower, s, -jnp.inf) if masked else s
        for r, _ in chains:
            m_new[r] = jnp.maximum(m_ref[r][...], jnp.max(ss[r], axis=-1, keepdims=True))
        for r, j in chains:
            start = pl.multiple_of(j * tb, tb)
            alpha = jnp.exp2(m_ref[r][...] - m_new[r])
            p = jnp.exp2(ss[r] - jnp.concatenate([m_new[r]] * (tb // LANES), axis=1))
            l_ref[r][...] = alpha * l_ref[r][...] + jnp.sum(p, axis=-1, keepdims=True)
            m_ref[r][...] = m_new[r]
            pv = jnp.dot(p.astype(BF16), v_ref[pl.ds(start, tb), :],
                         preferred_element_type=F32)
            acc_ref[r][...] = alpha * acc_ref[r][...] + pv

    for r in range(nsub):
        m_ref[r][...] = jnp.full((tb, LANES), -jnp.inf, F32)
        l_ref[r][...] = jnp.zeros((tb, LANES), F32)
        acc_ref[r][...] = jnp.zeros((tb, MLA_D_V), F32)
    for d in range(nsub):
        steps([(r, first + r - d) for r in range(d, nsub)], d == 0)

    def body(j, carry):
        steps([(r, j) for r in range(nsub)], False)
        return carry

    lax.fori_loop(0, first, body, 0)
    for r in range(nsub):
        o_ref[r * tb:(r + 1) * tb, :] = (acc_ref[r][...] / l_ref[r][...]).astype(o_ref.dtype)


def _mla_attention(q, k, v, *, tb, nsub):
    s = q.shape[0]
    tq = tb * nsub
    return pl.pallas_call(
        functools.partial(_mla_attn_kernel, tb=tb, nsub=nsub),
        out_shape=jax.ShapeDtypeStruct((s, MLA_HEADS * MLA_D_V), BF16),
        grid=(MLA_HEADS, s // tq),
        in_specs=[pl.BlockSpec((tq, MLA_D_QK_PAD), lambda h, i: (i, h)),
                  pl.BlockSpec((s, MLA_D_QK_PAD), lambda h, i: (0, h)),
                  pl.BlockSpec((s, MLA_D_V), lambda h, i: (0, h))],
        out_specs=pl.BlockSpec((tq, MLA_D_V), lambda h, i: (i, h)),
        scratch_shapes=([pltpu.VMEM((tb, LANES), F32)] * (2 * nsub)
                        + [pltpu.VMEM((tb, MLA_D_V), F32)] * nsub),
        compiler_params=_params("arbitrary", "arbitrary"),
        name="mla_attention",
    )(q, k, v)


def _sb_attn_kernel(q_ref, k_ref, v_ref, o_ref, carry_ref, acc_ref, knpm_ref, *, tb, nsub):
    i = pl.program_id(1)
    first = i * nsub
    strict = (lax.broadcasted_iota(jnp.int32, (tb, tb), 1)
              < lax.broadcasted_iota(jnp.int32, (tb, tb), 0))
    tri2 = ((lax.broadcasted_iota(jnp.int32, (2 * tb, tb), 0) & (tb - 1))
            >= lax.broadcasted_iota(jnp.int32, (2 * tb, tb), 1)).astype(BF16)

    def steps(chains, masked):
        zs, hls, cums, as_ = {}, {}, {}, {}

        def qk(r, j):
            zs[r] = _qk(q_ref[r * tb:(r + 1) * tb, :],
                        k_ref[pl.ds(pl.multiple_of(j * tb, tb), tb), :])

        def softplus(r, j):
            z = zs[r]
            neg_abs = lax.bitcast_convert_type(
                lax.bitcast_convert_type(z, jnp.uint32) | jnp.uint32(0x80000000), F32)
            sp = jnp.maximum(z, 0.0) + jnp.log2(1.0 + jnp.exp2(neg_abs))
            sp = jnp.where(strict, sp, 0.0) if masked else sp
            hi = sp.astype(BF16)
            lo = (sp - hi.astype(F32)).astype(BF16)
            hls[r] = jnp.concatenate([hi, lo], axis=1)

        def cumsum(r, j):
            cums[r] = jnp.dot(hls[r], tri2, preferred_element_type=F32)

        def weights(r, j):
            a = jnp.exp2(zs[r] - (cums[r] + carry_ref[r]))
            as_[r] = (jnp.where(strict, a, 0.0) if masked else a).astype(BF16)
            carry_ref[r] += cums[r][:, 0:1]

        def av(r, j):
            acc_ref[r] += jnp.dot(as_[r], v_ref[pl.ds(pl.multiple_of(j * tb, tb), tb), :],
                                  preferred_element_type=F32)

        for stage in (qk, softplus, cumsum, weights, av):
            for r, j in chains:
                stage(r, j)

    @pl.when(i == 0)
    def _():
        def tile_norm(j, running):
            kf = k_ref[pl.ds(pl.multiple_of(j * tb, tb), tb), :].astype(F32)
            kn2 = jnp.max(jnp.sum(kf * kf, axis=-1, keepdims=True), axis=0, keepdims=True)
            running = jnp.maximum(running, kn2)
            knpm_ref[pl.ds(j, 1), :] = jnp.broadcast_to(running, (1, LANES))
            return running

        lax.fori_loop(0, k_ref.shape[0] // tb, tile_norm, jnp.zeros((1, 1), F32))

    qf = q_ref[...].astype(F32)
    qn2 = jnp.max(jnp.sum(qf * qf, axis=-1, keepdims=True), axis=0, keepdims=True)

    def may_contribute(j, blocks=tuple(range(nsub))):
        cmin = carry_ref[blocks[0]]
        for r in blocks[1:]:
            cmin = jnp.minimum(cmin, carry_ref[r])
        slack = jnp.min(cmin, axis=0, keepdims=True) - SB_UNDERFLOW_LOG2
        kn2 = knpm_ref[pl.ds(jnp.maximum(j, 0), 1), :]
        live = jnp.logical_or(slack <= 0.0, qn2 * kn2 * 1.01 >= slack * slack)
        return (jnp.max(jnp.where(live, 1.0, 0.0)) > 0.5).astype(jnp.int32)

    carry_ref[...] = jnp.zeros(carry_ref.shape, F32)
    acc_ref[...] = jnp.zeros(acc_ref.shape, F32)
    for d in range(nsub):
        chains = [(r, first + r - d) for r in range(d, nsub)]
        if d < 2:
            steps(chains, d == 0)
        else:
            blocks = tuple(r for r, _ in chains)
            pl.when(may_contribute(first + nsub - 1 - d, blocks) > 0)(
                functools.partial(steps, chains, False))

    def cond(state):
        jj, live = state
        return jnp.logical_and(jj < first, live > 0)

    def body(state):
        jj, _ = state
        j = first - 1 - jj
        steps([(r, j) for r in range(nsub)], False)
        return jj + 1, may_contribute(j - 1)

    lax.while_loop(cond, body, (jnp.int32(0), may_contribute(first - 1)))
    o_ref[...] = acc_ref[...].reshape(o_ref.shape).astype(o_ref.dtype)


def _sb_attention(qkv, *, tb, nsub):
    s = qkv.shape[0]
    hd = SB_HEAD_DIM
    tq = tb * nsub
    return pl.pallas_call(
        functools.partial(_sb_attn_kernel, tb=tb, nsub=nsub),
        out_shape=jax.ShapeDtypeStruct((s, SB_HEADS * hd), BF16),
        grid=(SB_HEADS, s // tq),
        in_specs=[pl.BlockSpec((tq, hd), lambda h, i: (i, h)),
                  pl.BlockSpec((s, hd), lambda h, i: (0, SB_HEADS + h)),
                  pl.BlockSpec((s, hd), lambda h, i: (0, 2 * SB_HEADS + h))],
        out_specs=pl.BlockSpec((tq, hd), lambda h, i: (i, h)),
        scratch_shapes=[pltpu.VMEM((nsub, tb, 1), F32), pltpu.VMEM((nsub, tb, hd), F32),
                        pltpu.VMEM((s // tb, LANES), F32)],
        compiler_params=_params("arbitrary", "arbitrary"),
        name="sb_attention",
    )(qkv, qkv, qkv)


def _pool_kernel(x_ref, halo_ref, g_ref, sh_ref, sc_ref, w_ref, ps_ref, gate_ref,
                 o_ref, hbuf_ref, *, tm):
    i = pl.program_id(0)
    g, sh, sc = g_ref[...], sh_ref[...], sc_ref[...]
    x = x_ref[...]
    h = _modulated(x, g, sh, sc)
    h_halo = _modulated(halo_ref[...], g, sh, sc)
    hbuf_ref[:POOL_HALO, :] = jnp.where(i > 0, h_halo, 0.0)
    hbuf_ref[POOL_HALO:, :] = h
    t = i * tm + lax.broadcasted_iota(jnp.int32, (tm, 1), 0)
    for gi, w in enumerate(POOL_WINDOWS):
        lo, hi = gi * POOL_GROUP, (gi + 1) * POOL_GROUP
        ssum = h[:, lo:hi]
        for k in range(1, w):
            ssum = ssum + hbuf_ref[POOL_HALO - k:POOL_HALO - k + tm, lo:hi]
        cnt = jnp.minimum(t + 1, w).astype(F32)
        p = ssum / cnt - h[:, lo:hi]
        y = jnp.dot(p.astype(BF16), w_ref[gi], preferred_element_type=F32)
        o_ref[:, lo:hi] = x[:, lo:hi] + gate_ref[:, lo:hi] * (y * ps_ref[:, lo:hi])


def _pool_layer(x, gain, shift, scale, w_pool, pool_scale, gate):
    s, d = x.shape
    tm = 256
    ng = len(POOL_WINDOWS)
    vec = pl.BlockSpec((1, d), lambda i: (0, 0))
    halo_blocks = tm // POOL_HALO
    return pl.pallas_call(
        functools.partial(_pool_kernel, tm=tm),
        out_shape=jax.ShapeDtypeStruct((s, d), F32),
        grid=(s // tm,),
        in_specs=[pl.BlockSpec((tm, d), lambda i: (i, 0)),
                  pl.BlockSpec((POOL_HALO, d),
                               lambda i: (jnp.maximum(i * halo_blocks - 1, 0), 0)),
                  vec, vec, vec,
                  pl.BlockSpec((ng, POOL_GROUP, POOL_GROUP), lambda i: (0, 0, 0)),
                  vec, vec],
        out_specs=pl.BlockSpec((tm, d), lambda i: (i, 0)),
        scratch_shapes=[pltpu.VMEM((tm + POOL_HALO, d), F32)],
        compiler_params=_params("arbitrary"),
        name="pool_layer",
    )(x, x, gain, shift, scale, w_pool, pool_scale, gate)


def _pad_cols(w, n):
    return jnp.pad(w, ((0, 0), (0, n - w.shape[1])))


def _mla_weights(w_dq, w_uq, w_dkv, w_ukv, w_o):
    h = MLA_HEADS
    w_uq_h = w_uq.reshape(MLA_Q_LORA, h, MLA_D_NOPE + MLA_D_ROPE)
    w_uq_pad = jnp.pad(w_uq_h, ((0, 0), (0, 0), (0, MLA_D_QK_PAD - MLA_D_NOPE - MLA_D_ROPE)))
    w_uq_pad = w_uq_pad.reshape(MLA_Q_LORA, h * MLA_D_QK_PAD).astype(BF16)
    w_ukv_h = w_ukv.reshape(MLA_KV_LORA, h, MLA_D_NOPE + MLA_D_V)
    w_uk = w_ukv_h[:, :, :MLA_D_NOPE].reshape(MLA_KV_LORA, h * MLA_D_NOPE).astype(BF16)
    w_uv = w_ukv_h[:, :, MLA_D_NOPE:].reshape(MLA_KV_LORA, h * MLA_D_V).astype(BF16)
    w_dkv_pad = _pad_cols(w_dkv, MLA_KVA_PAD).astype(BF16)
    return w_dq.astype(BF16), w_uq_pad, w_dkv_pad, w_uk, w_uv, w_o.astype(BF16)


def _mla_layer(x, h, rope, w_dq, q_norm, w_uq, w_dkv, kv_norm, w_ukv, w_o, gate):
    w_dq_b, w_uq_pad, w_dkv_pad, w_uk, w_uv, w_o_b = _mla_weights(w_dq, w_uq, w_dkv, w_ukv, w_o)
    cq, ckv, k_rope = _mla_down(h, w_dq_b, w_dkv_pad, q_norm[None, :], kv_norm[None, :], *rope)
    q = _mla_q_up(cq, w_uq_pad, *rope)
    k, v = _mla_kv_up(ckv, w_uk, w_uv, k_rope)
    o = _mla_attention(q, k, v, tb=1024, nsub=2)
    return _mm_resid(o, w_o_b, x, gate, tm=1024, tn=1024, tk=o.shape[1], name="mla_out")


def _sb_layer(x, h, w_qkv, w_o, gate):
    n = w_qkv.shape[1]
    colscale = jnp.where(jnp.arange(n) < SB_HEADS * SB_HEAD_DIM,
                         SB_HEAD_DIM ** -0.5 * LOG2_E, 1.0).astype(F32)[None, :]
    qkv = _mm_scale(h, w_qkv.astype(BF16), colscale, tm=1024, tn=1024, name="sb_qkv")
    o = _sb_attention(qkv, tb=256, nsub=4)
    return _mm_resid(o, w_o.astype(BF16), x, gate, tm=1024, tn=1024, tk=o.shape[1],
                     name="sb_out")


def _cast_pad_cols_kernel(w_ref, o_ref):
    n = w_ref.shape[1]
    o_ref[:, :n] = w_ref[...].astype(o_ref.dtype)
    o_ref[:, n:] = jnp.zeros((o_ref.shape[0], o_ref.shape[1] - n), o_ref.dtype)


def _cast_pad_cols(w, n_pad):
    nl, k, n = w.shape
    tr = 256
    return pl.pallas_call(
        _cast_pad_cols_kernel,
        out_shape=jax.ShapeDtypeStruct((nl, k, n_pad), BF16),
        grid=(nl, k // tr),
        in_specs=[pl.BlockSpec((None, tr, n), lambda l, i: (l, i, 0))],
        out_specs=pl.BlockSpec((None, tr, n_pad), lambda l, i: (l, i, 0)),
        compiler_params=_params("arbitrary", "arbitrary"),
        name="cast_pad_cols",
    )(w)


def _cast_pad_rows_kernel(w_ref, o_ref, *, n_in_blocks):
    i = pl.program_id(1)

    @pl.when(i < n_in_blocks)
    def _():
        o_ref[...] = w_ref[...].astype(o_ref.dtype)

    @pl.when(i >= n_in_blocks)
    def _():
        o_ref[...] = jnp.zeros(o_ref.shape, o_ref.dtype)


def _cast_pad_rows(w, k_pad):
    nl, k, n = w.shape
    tr = 256
    n_in = k // tr
    return pl.pallas_call(
        functools.partial(_cast_pad_rows_kernel, n_in_blocks=n_in),
        out_shape=jax.ShapeDtypeStruct((nl, k_pad, n), BF16),
        grid=(nl, k_pad // tr),
        in_specs=[pl.BlockSpec((None, tr, n), lambda l, i: (l, jnp.minimum(i, n_in - 1), 0))],
        out_specs=pl.BlockSpec((None, tr, n), lambda l, i: (l, i, 0)),
        compiler_params=_params("arbitrary", "arbitrary"),
        name="cast_pad_rows",
    )(w)


def _ffn_weights(w_gate, w_up, w_down):
    return (_cast_pad_cols(w_gate, D_FF_PAD), _cast_pad_cols(w_up, D_FF_PAD),
            _cast_pad_rows(w_down, D_FF_PAD))


def _ffn_layer(x, h, ffn_w, layer, gate):
    wg, wu, wd = ffn_w
    act = _ffn_up(h, wg, wu, layer, tm=1024, tn=512)
    return _mm_resid(act, wd, x, gate, tm=512, tn=512, tk=D_FF_PAD, name="ffn_down",
                     layer=layer)


def kernel(x, c, positions, ada_w, ada_table, norm_mix, norm_ffn, norm_final,
           mla_w_dq, mla_q_norm, mla_w_uq, mla_w_dkv, mla_kv_norm, mla_w_ukv, mla_w_o,
           sb_w_qkv, sb_w_o, pool_w, pool_scale, ffn_w_gate, ffn_w_up, ffn_w_down):
    b, s, d = x.shape
    assert b == 1 and c.shape == (1, d)
    xs = x.reshape(s, d)
    mod = _ada_mod(c, ada_w, ada_table).reshape(DEPTH, N_MOD, 1, d)
    rope = _rope_tables(positions.reshape(s))
    ffn_w = _ffn_weights(ffn_w_gate, ffn_w_up, ffn_w_down)
    for i in range(DEPTH):
        shift_m, scale_m, gate_m = mod[i, 0], mod[i, 1], mod[i, 2]
        shift_f, scale_f, gate_f = mod[i, 3], mod[i, 4], mod[i, 5]
        gain_m = norm_mix[i][None, :]
        kind, j = i % N_MIXERS, i // N_MIXERS
        if kind == 0:
            h = _modulate(xs, gain_m, shift_m, scale_m)
            xs = _mla_layer(xs, h, rope, mla_w_dq[j], mla_q_norm[j], mla_w_uq[j],
                            mla_w_dkv[j], mla_kv_norm[j], mla_w_ukv[j], mla_w_o[j], gate_m)
        elif kind == 1:
            h = _modulate(xs, gain_m, shift_m, scale_m)
            xs = _sb_layer(xs, h, sb_w_qkv[j], sb_w_o[j], gate_m)
        else:
            xs = _pool_layer(xs, gain_m, shift_m, scale_m, pool_w[j].astype(BF16),
                             pool_scale[j][None, :], gate_m)
        h = _modulate(xs, norm_ffn[i][None, :], shift_f, scale_f)
        xs = _ffn_layer(xs, h, ffn_w, i, gate_f)
    return _final_norm(xs, norm_final[None, :]).reshape(b, s, d)
```

```python
import functools

import jax
import jax.numpy as jnp
from jax import lax
from jax.experimental import pallas as pl
from jax.experimental.pallas import tpu as pltpu

F32 = jnp.float32
BF16 = jnp.bfloat16

D_MODEL = 4096
SEQ = 16384
DEPTH = 4
N_MIXERS = 3
RMS_EPS = 1e-6
N_MOD = 6

MLA_HEADS = 16
MLA_Q_LORA = 1536
MLA_KV_LORA = 512
MLA_D_NOPE = 128
MLA_D_ROPE = 64
MLA_D_V = 128
ROPE_THETA = 10000.0
MLA_D_QK_PAD = 256
MLA_KVA_PAD = MLA_KV_LORA + 128

SB_HEADS = 16
SB_HEAD_DIM = 128

POOL_WINDOWS = (2, 4, 8, 16)
POOL_GROUP = D_MODEL // len(POOL_WINDOWS)
POOL_HALO = 16

D_FF = ((8 * D_MODEL + 3 * 256 - 1) // (3 * 256)) * 256
D_FF_PAD = 11264

LANES = 128
LOG2_E = 1.4426950408889634
SB_UNDERFLOW_LOG2 = 150.0
VMEM_LIMIT = 56 * 1024 * 1024


def _params(*sem):
    return pltpu.CompilerParams(dimension_semantics=sem, vmem_limit_bytes=VMEM_LIMIT)


def _ada_kernel(c_ref, w_ref, tab_ref, o_ref):
    c = c_ref[...]
    s = c * jax.nn.sigmoid(c)
    lhs = jnp.broadcast_to(s, (8, s.shape[1])).astype(BF16)
    base = jnp.dot(lhs, w_ref[...].astype(BF16), preferred_element_type=F32)
    o_ref[...] = base[0:1, :] + tab_ref[...]


def _ada_mod(c, ada_w, ada_table):
    d = c.shape[1]
    n = ada_w.shape[1]
    tn = 512
    tab = ada_table.reshape(DEPTH, n)
    return pl.pallas_call(
        _ada_kernel,
        out_shape=jax.ShapeDtypeStruct((DEPTH, n), F32),
        grid=(n // tn,),
        in_specs=[
            pl.BlockSpec((1, d), lambda j: (0, 0)),
            pl.BlockSpec((d, tn), lambda j: (0, j)),
            pl.BlockSpec((DEPTH, tn), lambda j: (0, j)),
        ],
        out_specs=pl.BlockSpec((DEPTH, tn), lambda j: (0, j)),
        compiler_params=_params("arbitrary"),
        name="ada_mod",
    )(c, ada_w, tab)


def _modulated(x, gain, shift, scale):
    ms = jnp.mean(x * x, axis=-1, keepdims=True)
    y = x * lax.rsqrt(ms + RMS_EPS) * gain
    return y * (1.0 + scale) + shift


def _modulate_kernel(x_ref, g_ref, sh_ref, sc_ref, o_ref):
    h = _modulated(x_ref[...], g_ref[...], sh_ref[...], sc_ref[...])
    o_ref[...] = h.astype(o_ref.dtype)


def _modulate(x, gain, shift, scale):
    s, d = x.shape
    tm = 512
    vec = pl.BlockSpec((1, d), lambda i: (0, 0))
    return pl.pallas_call(
        _modulate_kernel,
        out_shape=jax.ShapeDtypeStruct((s, d), BF16),
        grid=(s // tm,),
        in_specs=[pl.BlockSpec((tm, d), lambda i: (i, 0)), vec, vec, vec],
        out_specs=pl.BlockSpec((tm, d), lambda i: (i, 0)),
        compiler_params=_params("arbitrary"),
        name="modulate",
    )(x, gain, shift, scale)


def _final_norm_kernel(x_ref, g_ref, o_ref):
    x = x_ref[...]
    ms = jnp.mean(x * x, axis=-1, keepdims=True)
    o_ref[...] = x * lax.rsqrt(ms + RMS_EPS) * g_ref[...]


def _final_norm(x, gain):
    s, d = x.shape
    tm = 256
    return pl.pallas_call(
        _final_norm_kernel,
        out_shape=jax.ShapeDtypeStruct((s, d), F32),
        grid=(s // tm,),
        in_specs=[pl.BlockSpec((tm, d), lambda i: (i, 0)),
                  pl.BlockSpec((1, d), lambda i: (0, 0))],
        out_specs=pl.BlockSpec((tm, d), lambda i: (i, 0)),
        compiler_params=_params("arbitrary"),
        name="final_norm",
    )(x, gain)


def _mm_resid_kernel(a_ref, w_ref, x_ref, g_ref, o_ref, acc_ref, *, nk):
    k = pl.program_id(2)
    part = jnp.dot(a_ref[...], w_ref[...], preferred_element_type=F32)
    if nk == 1:
        o_ref[...] = x_ref[...] + g_ref[...] * part
        return

    @pl.when(k == 0)
    def _():
        acc_ref[...] = part

    @pl.when(jnp.logical_and(k > 0, k < nk - 1))
    def _():
        acc_ref[...] += part

    @pl.when(k == nk - 1)
    def _():
        o_ref[...] = x_ref[...] + g_ref[...] * (acc_ref[...] + part)


def _mm_resid(a, w, x, gate, *, tm, tn, tk, name, layer=None):
    m, kdim = a.shape
    n = w.shape[-1]
    nk = kdim // tk
    if layer is None:
        wspec = pl.BlockSpec((tk, tn), lambda i, j, k: (k, j))
    else:
        wspec = pl.BlockSpec((None, tk, tn), lambda i, j, k: (layer, k, j))
    return pl.pallas_call(
        functools.partial(_mm_resid_kernel, nk=nk),
        out_shape=jax.ShapeDtypeStruct((m, n), F32),
        grid=(m // tm, n // tn, nk),
        in_specs=[
            pl.BlockSpec((tm, tk), lambda i, j, k: (i, k)),
            wspec,
            pl.BlockSpec((tm, tn), lambda i, j, k: (i, j)),
            pl.BlockSpec((1, tn), lambda i, j, k: (0, j)),
        ],
        out_specs=pl.BlockSpec((tm, tn), lambda i, j, k: (i, j)),
        scratch_shapes=[pltpu.VMEM((tm, tn), F32)],
        compiler_params=_params("arbitrary", "arbitrary", "arbitrary"),
        name=name,
    )(a, w, x, gate)


def _mm_scale_kernel(a_ref, w_ref, s_ref, o_ref):
    acc = jnp.dot(a_ref[...], w_ref[...], preferred_element_type=F32)
    o_ref[...] = (acc * s_ref[...]).astype(o_ref.dtype)


def _mm_scale(a, w, colscale, *, tm, tn, name):
    m, kdim = a.shape
    n = w.shape[1]
    return pl.pallas_call(
        _mm_scale_kernel,
        out_shape=jax.ShapeDtypeStruct((m, n), BF16),
        grid=(m // tm, n // tn),
        in_specs=[
            pl.BlockSpec((tm, kdim), lambda i, j: (i, 0)),
            pl.BlockSpec((kdim, tn), lambda i, j: (0, j)),
            pl.BlockSpec((1, tn), lambda i, j: (0, j)),
        ],
        out_specs=pl.BlockSpec((tm, tn), lambda i, j: (i, j)),
        compiler_params=_params("arbitrary", "arbitrary"),
        name=name,
    )(a, w, colscale)


def _ffn_up_kernel(a_ref, wg_ref, wu_ref, o_ref):
    a = a_ref[...]
    g = jnp.dot(a, wg_ref[...], preferred_element_type=F32)
    u = jnp.dot(a, wu_ref[...], preferred_element_type=F32)
    o_ref[...] = (g * jax.nn.sigmoid(g) * u).astype(o_ref.dtype)


def _ffn_up(h, wg, wu, layer, *, tm, tn):
    m, kdim = h.shape
    n = wg.shape[-1]
    wspec = pl.BlockSpec((None, kdim, tn), lambda i, j: (layer, 0, j))
    return pl.pallas_call(
        _ffn_up_kernel,
        out_shape=jax.ShapeDtypeStruct((m, n), BF16),
        grid=(m // tm, n // tn),
        in_specs=[pl.BlockSpec((tm, kdim), lambda i, j: (i, 0)), wspec, wspec],
        out_specs=pl.BlockSpec((tm, tn), lambda i, j: (i, j)),
        compiler_params=_params("arbitrary", "arbitrary"),
        name="ffn_up",
    )(h, wg, wu)


def _rope_table_kernel(pos_ref, inv_ref, c_ref, s1_ref, s2_ref):
    ang = pos_ref[...] * inv_ref[...]
    cos = jnp.cos(ang)
    sin = jnp.sin(ang)
    lane = lax.broadcasted_iota(jnp.int32, ang.shape, 1)
    half = MLA_D_ROPE // 2
    c_ref[...] = jnp.where(lane < MLA_D_ROPE, cos, 0.0)
    s1_ref[...] = jnp.where(lane < half, -sin, 0.0)
    s2_ref[...] = jnp.where(jnp.logical_and(lane >= half, lane < MLA_D_ROPE), sin, 0.0)


def _rope_tables(positions):
    s = positions.shape[0]
    half = MLA_D_ROPE // 2
    inv = 1.0 / (ROPE_THETA ** (jnp.arange(0, MLA_D_ROPE, 2, dtype=F32) / MLA_D_ROPE))
    inv_lanes = jnp.concatenate([inv, inv, jnp.zeros((LANES - 2 * half,), F32)])[None, :]
    pos = positions.astype(F32)[:, None]
    tm = 1024
    out = jax.ShapeDtypeStruct((s, LANES), F32)
    tab = pl.BlockSpec((tm, LANES), lambda i: (i, 0))
    return pl.pallas_call(
        _rope_table_kernel,
        out_shape=(out, out, out),
        grid=(s // tm,),
        in_specs=[pl.BlockSpec((tm, 1), lambda i: (i, 0)),
                  pl.BlockSpec((1, LANES), lambda i: (0, 0))],
        out_specs=(tab, tab, tab),
        compiler_params=_params("arbitrary"),
        name="rope_tables",
    )(pos, inv_lanes)


def _rotate_lanes(r, c, s1, s2):
    return r * c + pltpu.roll(r, 96, axis=1) * s1 + pltpu.roll(r, 32, axis=1) * s2


def _mla_down_kernel(a_ref, wq_ref, wkv_ref, qn_ref, kvn_ref, c_ref, s1_ref, s2_ref,
                     cq_ref, ckv_ref, kr_ref, accq_ref, acckv_ref, *, nk):
    k = pl.program_id(1)
    a = a_ref[...]
    pq = jnp.dot(a, wq_ref[...], preferred_element_type=F32)
    pkv = jnp.dot(a, wkv_ref[...], preferred_element_type=F32)

    @pl.when(k == 0)
    def _():
        accq_ref[...] = pq
        acckv_ref[...] = pkv

    @pl.when(k > 0)
    def _():
        accq_ref[...] += pq
        acckv_ref[...] += pkv

    @pl.when(k == nk - 1)
    def _():
        cq = accq_ref[...]
        ms = jnp.mean(cq * cq, axis=-1, keepdims=True)
        cq_ref[...] = (cq * lax.rsqrt(ms + RMS_EPS) * qn_ref[...]).astype(cq_ref.dtype)
        ckv = acckv_ref[:, :MLA_KV_LORA]
        ms = jnp.mean(ckv * ckv, axis=-1, keepdims=True)
        ckv_ref[...] = (ckv * lax.rsqrt(ms + RMS_EPS) * kvn_ref[...]).astype(ckv_ref.dtype)
        kr = acckv_ref[:, MLA_KV_LORA:]
        kr_ref[...] = _rotate_lanes(kr, c_ref[...], s1_ref[...], s2_ref[...]).astype(kr_ref.dtype)


def _mla_down(h, w_dq, w_dkv_pad, q_norm, kv_norm, rope_c, rope_s1, rope_s2):
    m, kdim = h.shape
    tm, tk = 1024, 1024
    nk = kdim // tk
    nq = w_dq.shape[1]
    nkv = w_dkv_pad.shape[1]
    tab = pl.BlockSpec((tm, LANES), lambda i, k: (i, 0))
    return pl.pallas_call(
        functools.partial(_mla_down_kernel, nk=nk),
        out_shape=(jax.ShapeDtypeStruct((m, nq), BF16),
                   jax.ShapeDtypeStruct((m, MLA_KV_LORA), BF16),
                   jax.ShapeDtypeStruct((m, LANES), BF16)),
        grid=(m // tm, nk),
        in_specs=[
            pl.BlockSpec((tm, tk), lambda i, k: (i, k)),
            pl.BlockSpec((tk, nq), lambda i, k: (k, 0)),
            pl.BlockSpec((tk, nkv), lambda i, k: (k, 0)),
            pl.BlockSpec((1, nq), lambda i, k: (0, 0)),
            pl.BlockSpec((1, MLA_KV_LORA), lambda i, k: (0, 0)),
            tab, tab, tab,
        ],
        out_specs=(pl.BlockSpec((tm, nq), lambda i, k: (i, 0)),
                   pl.BlockSpec((tm, MLA_KV_LORA), lambda i, k: (i, 0)),
                   pl.BlockSpec((tm, LANES), lambda i, k: (i, 0))),
        scratch_shapes=[pltpu.VMEM((tm, nq), F32), pltpu.VMEM((tm, nkv), F32)],
        compiler_params=_params("arbitrary", "arbitrary"),
        name="mla_down",
    )(h, w_dq, w_dkv_pad, q_norm, kv_norm, rope_c, rope_s1, rope_s2)


def _mla_q_up_kernel(a_ref, w_ref, c_ref, s1_ref, s2_ref, o_ref, *, heads_per_tile):
    acc = jnp.dot(a_ref[...], w_ref[...], preferred_element_type=F32)
    scale = (MLA_D_NOPE + MLA_D_ROPE) ** -0.5 * LOG2_E
    c, s1, s2 = c_ref[...], s1_ref[...], s2_ref[...]
    for hh in range(heads_per_tile):
        lo = hh * MLA_D_QK_PAD
        o_ref[:, lo:lo + LANES] = (acc[:, lo:lo + LANES] * scale).astype(o_ref.dtype)
        r = _rotate_lanes(acc[:, lo + LANES:lo + 2 * LANES], c, s1, s2)
        o_ref[:, lo + LANES:lo + 2 * LANES] = (r * scale).astype(o_ref.dtype)


def _mla_q_up(cq, w_uq_pad, rope_c, rope_s1, rope_s2):
    m, kdim = cq.shape
    n = w_uq_pad.shape[1]
    tm, tn = 512, 1024
    tab = pl.BlockSpec((tm, LANES), lambda i, j: (i, 0))
    return pl.pallas_call(
        functools.partial(_mla_q_up_kernel, heads_per_tile=tn // MLA_D_QK_PAD),
        out_shape=jax.ShapeDtypeStruct((m, n), BF16),
        grid=(m // tm, n // tn),
        in_specs=[pl.BlockSpec((tm, kdim), lambda i, j: (i, 0)),
                  pl.BlockSpec((kdim, tn), lambda i, j: (0, j)),
                  tab, tab, tab],
        out_specs=pl.BlockSpec((tm, tn), lambda i, j: (i, j)),
        compiler_params=_params("arbitrary", "arbitrary"),
        name="mla_q_up",
    )(cq, w_uq_pad, rope_c, rope_s1, rope_s2)


def _mla_kv_up_kernel(a_ref, wk_ref, wv_ref, kr_ref, k_ref, v_ref):
    a = a_ref[...]
    kn = jnp.dot(a, wk_ref[...], preferred_element_type=F32)
    vn = jnp.dot(a, wv_ref[...], preferred_element_type=F32)
    kr = kr_ref[...]
    ones_col = (lax.broadcasted_iota(jnp.int32, kr.shape, 1) == 0).astype(v_ref.dtype)
    for hh in range(MLA_HEADS):
        lo = hh * MLA_D_QK_PAD
        k_ref[:, lo:lo + LANES] = kn[:, hh * LANES:(hh + 1) * LANES].astype(k_ref.dtype)
        k_ref[:, lo + LANES:lo + 2 * LANES] = kr
        v_ref[:, lo:lo + LANES] = vn[:, hh * LANES:(hh + 1) * LANES].astype(v_ref.dtype)
        v_ref[:, lo + LANES:lo + 2 * LANES] = ones_col


def _mla_kv_up(ckv, w_uk, w_uv, k_rope):
    m, kdim = ckv.shape
    tm = 512
    nk = w_uk.shape[1]
    nv = w_uv.shape[1]
    return pl.pallas_call(
        _mla_kv_up_kernel,
        out_shape=(jax.ShapeDtypeStruct((m, MLA_HEADS * MLA_D_QK_PAD), BF16),
                   jax.ShapeDtypeStruct((m, MLA_HEADS * MLA_D_QK_PAD), BF16)),
        grid=(m // tm,),
        in_specs=[pl.BlockSpec((tm, kdim), lambda i: (i, 0)),
                  pl.BlockSpec((kdim, nk), lambda i: (0, 0)),
                  pl.BlockSpec((kdim, nv), lambda i: (0, 0)),
                  pl.BlockSpec((tm, LANES), lambda i: (i, 0))],
        out_specs=(pl.BlockSpec((tm, MLA_HEADS * MLA_D_QK_PAD), lambda i: (i, 0)),
                   pl.BlockSpec((tm, MLA_HEADS * MLA_D_QK_PAD), lambda i: (i, 0))),
        compiler_params=_params("arbitrary"),
        name="mla_kv_up",
    )(ckv, w_uk, w_uv, k_rope)


def _qk(q, k):
    return lax.dot_general(q, k, (((1,), (1,)), ((), ())), preferred_element_type=F32)


def _mla_attn_kernel(q_ref, k_ref, v_ref, o_ref, *scratch, tb, nsub):
    m_ref, acc_ref = scratch[:nsub], scratch[nsub:]
    i = pl.program_id(1)
    first = i * nsub
    lower = (lax.broadcasted_iota(jnp.int32, (tb, tb), 1)
             <= lax.broadcasted_iota(jnp.int32, (tb, tb), 0))

    def steps(chains, masked):
        ss, m_new = {}, {}
        for r, j in chains:
            start = pl.multiple_of(j * tb, tb)
            s = _qk(q_ref[r * tb:(r + 1) * tb, :], k_ref[pl.ds(start, tb), :])
            ss[r] = jnp.where(lower, s, -jnp.inf) if masked else s
        for r, _ in chains:
            m_new[r] = jnp.maximum(m_ref[r][...], jnp.max(ss[r], axis=-1, keepdims=True))
        for r, j in chains:
            start = pl.multiple_of(j * tb, tb)
            alpha = jnp.exp2(m_ref[r][...] - m_new[r])
            p = jnp.exp2(ss[r] - jnp.concatenate([m_new[r]] * (tb // LANES), axis=1))
            m_ref[r][...] = m_new[r]
            pv = jnp.dot(p.astype(BF16), v_ref[pl.ds(start, tb), :],
                         preferred_element_type=F32)
            acc_ref[r][...] = jnp.concatenate([alpha, alpha], axis=1) * acc_ref[r][...] + pv

    for r in range(nsub):
        m_ref[r][...] = jnp.full((tb, LANES), -jnp.inf, F32)
        acc_ref[r][...] = jnp.zeros((tb, MLA_D_QK_PAD), F32)
    for d in range(nsub):
        steps([(r, first + r - d) for r in range(d, nsub)], d == 0)

    def body(j, carry):
        steps([(r, j) for r in range(nsub)], False)
        return carry

    lax.fori_loop(0, first, body, 0)
    for r in range(nsub):
        acc = acc_ref[r][...]
        o_ref[r * tb:(r + 1) * tb, :] = (acc[:, :MLA_D_V]
                                         / acc[:, MLA_D_V:MLA_D_V + 1]).astype(o_ref.dtype)


def _mla_attention(q, k, v, *, tb, nsub):
    s = q.shape[0]
    tq = tb * nsub
    return pl.pallas_call(
        functools.partial(_mla_attn_kernel, tb=tb, nsub=nsub),
        out_shape=jax.ShapeDtypeStruct((s, MLA_HEADS * MLA_D_V), BF16),
        grid=(MLA_HEADS, s // tq),
        in_specs=[pl.BlockSpec((tq, MLA_D_QK_PAD), lambda h, i: (i, h)),
                  pl.BlockSpec((s, MLA_D_QK_PAD), lambda h, i: (0, h)),
                  pl.BlockSpec((s, MLA_D_QK_PAD), lambda h, i: (0, h))],
        out_specs=pl.BlockSpec((tq, MLA_D_V), lambda h, i: (i, h)),
        scratch_shapes=([pltpu.VMEM((tb, LANES), F32)] * nsub
                        + [pltpu.VMEM((tb, MLA_D_QK_PAD), F32)] * nsub),
        compiler_params=_params("arbitrary", "arbitrary"),
        name="mla_attention",
    )(q, k, v)


def _sb_attn_kernel(q_ref, k_ref, v_ref, o_ref, carry_ref, acc_ref, knpm_ref, *, tb, nsub):
    i = pl.program_id(1)
    first = i * nsub
    strict = (lax.broadcasted_iota(jnp.int32, (tb, tb), 1)
              < lax.broadcasted_iota(jnp.int32, (tb, tb), 0))
    tri2 = ((lax.broadcasted_iota(jnp.int32, (2 * tb, tb), 0) & (tb - 1))
            >= lax.broadcasted_iota(jnp.int32, (2 * tb, tb), 1)).astype(BF16)

    def steps(chains, masked):
        zs, hls, cums, as_ = {}, {}, {}, {}

        def qk(r, j):
            zs[r] = _qk(q_ref[r * tb:(r + 1) * tb, :],
                        k_ref[pl.ds(pl.multiple_of(j * tb, tb), tb), :])

        def softplus(r, j):
            z = zs[r]
            neg_abs = lax.bitcast_convert_type(
                lax.bitcast_convert_type(z, jnp.uint32) | jnp.uint32(0x80000000), F32)
            sp = jnp.maximum(z, 0.0) + jnp.log2(1.0 + jnp.exp2(neg_abs))
            sp = jnp.where(strict, sp, 0.0) if masked else sp
            hi = sp.astype(BF16)
            lo = (sp - hi.astype(F32)).astype(BF16)
            hls[r] = jnp.concatenate([hi, lo], axis=1)

        def cumsum(r, j):
            cums[r] = jnp.dot(hls[r], tri2, preferred_element_type=F32)

        def weights(r, j):
            a = jnp.exp2(zs[r] - (cums[r] + carry_ref[r]))
            as_[r] = (jnp.where(strict, a, 0.0) if masked else a).astype(BF16)
            carry_ref[r] += cums[r][:, 0:1]

        def av(r, j):
            acc_ref[r] += jnp.dot(as_[r], v_ref[pl.ds(pl.multiple_of(j * tb, tb), tb), :],
                                  preferred_element_type=F32)

        for stage in (qk, softplus, cumsum, weights, av):
            for r, j in chains:
                stage(r, j)

    @pl.when(i == 0)
    def _():
        def tile_norm(j, running):
            kf = k_ref[pl.ds(pl.multiple_of(j * tb, tb), tb), :].astype(F32)
            kn2 = jnp.max(jnp.sum(kf * kf, axis=-1, keepdims=True), axis=0, keepdims=True)
            running = jnp.maximum(running, kn2)
            knpm_ref[pl.ds(j, 1), :] = jnp.broadcast_to(running, (1, LANES))
            return running

        lax.fori_loop(0, k_ref.shape[0] // tb, tile_norm, jnp.zeros((1, 1), F32))

    qf = q_ref[...].astype(F32)
    qn2 = jnp.max(jnp.sum(qf * qf, axis=-1, keepdims=True), axis=0, keepdims=True)

    def may_contribute(j, blocks=tuple(range(nsub))):
        cmin = carry_ref[blocks[0]]
        for r in blocks[1:]:
            cmin = jnp.minimum(cmin, carry_ref[r])
        slack = jnp.min(cmin, axis=0, keepdims=True) - SB_UNDERFLOW_LOG2
        kn2 = knpm_ref[pl.ds(jnp.maximum(j, 0), 1), :]
        live = jnp.logical_or(slack <= 0.0, qn2 * kn2 * 1.01 >= slack * slack)
        return (jnp.max(jnp.where(live, 1.0, 0.0)) > 0.5).astype(jnp.int32)

    carry_ref[...] = jnp.zeros(carry_ref.shape, F32)
    acc_ref[...] = jnp.zeros(acc_ref.shape, F32)
    for d in range(nsub):
        chains = [(r, first + r - d) for r in range(d, nsub)]
        if d < 2:
            steps(chains, d == 0)
        else:
            blocks = tuple(r for r, _ in chains)
            pl.when(may_contribute(first + nsub - 1 - d, blocks) > 0)(
                functools.partial(steps, chains, False))

    def cond(state):
        jj, live = state
        return jnp.logical_and(jj < first, live > 0)

    def body(state):
        jj, _ = state
        j = first - 1 - jj
        steps([(r, j) for r in range(nsub)], False)
        return jj + 1, may_contribute(j - 1)

    lax.while_loop(cond, body, (jnp.int32(0), may_contribute(first - 1)))
    o_ref[...] = acc_ref[...].reshape(o_ref.shape).astype(o_ref.dtype)


def _sb_attention(qkv, *, tb, nsub):
    s = qkv.shape[0]
    hd = SB_HEAD_DIM
    tq = tb * nsub
    return pl.pallas_call(
        functools.partial(_sb_attn_kernel, tb=tb, nsub=nsub),
        out_shape=jax.ShapeDtypeStruct((s, SB_HEADS * hd), BF16),
        grid=(SB_HEADS, s // tq),
        in_specs=[pl.BlockSpec((tq, hd), lambda h, i: (i, h)),
                  pl.BlockSpec((s, hd), lambda h, i: (0, SB_HEADS + h)),
                  pl.BlockSpec((s, hd), lambda h, i: (0, 2 * SB_HEADS + h))],
        out_specs=pl.BlockSpec((tq, hd), lambda h, i: (i, h)),
        scratch_shapes=[pltpu.VMEM((nsub, tb, 1), F32), pltpu.VMEM((nsub, tb, hd), F32),
                        pltpu.VMEM((s // tb, LANES), F32)],
        compiler_params=_params("arbitrary", "arbitrary"),
        name="sb_attention",
    )(qkv, qkv, qkv)


def _pool_kernel(x_ref, halo_ref, g_ref, sh_ref, sc_ref, w_ref, ps_ref, gate_ref,
                 o_ref, hbuf_ref, *, tm):
    i = pl.program_id(0)
    g, sh, sc = g_ref[...], sh_ref[...], sc_ref[...]
    x = x_ref[...]
    h = _modulated(x, g, sh, sc)
    h_halo = _modulated(halo_ref[...], g, sh, sc)
    hbuf_ref[:POOL_HALO, :] = jnp.where(i > 0, h_halo, 0.0)
    hbuf_ref[POOL_HALO:, :] = h
    t = i * tm + lax.broadcasted_iota(jnp.int32, (tm, 1), 0)
    for gi, w in enumerate(POOL_WINDOWS):
        lo, hi = gi * POOL_GROUP, (gi + 1) * POOL_GROUP
        ssum = h[:, lo:hi]
        for k in range(1, w):
            ssum = ssum + hbuf_ref[POOL_HALO - k:POOL_HALO - k + tm, lo:hi]
        cnt = jnp.minimum(t + 1, w).astype(F32)
        p = ssum / cnt - h[:, lo:hi]
        y = jnp.dot(p.astype(BF16), w_ref[gi], preferred_element_type=F32)
        o_ref[:, lo:hi] = x[:, lo:hi] + gate_ref[:, lo:hi] * (y * ps_ref[:, lo:hi])


def _pool_layer(x, gain, shift, scale, w_pool, pool_scale, gate):
    s, d = x.shape
    tm = 256
    ng = len(POOL_WINDOWS)
    vec = pl.BlockSpec((1, d), lambda i: (0, 0))
    halo_blocks = tm // POOL_HALO
    return pl.pallas_call(
        functools.partial(_pool_kernel, tm=tm),
        out_shape=jax.ShapeDtypeStruct((s, d), F32),
        grid=(s // tm,),
        in_specs=[pl.BlockSpec((tm, d), lambda i: (i, 0)),
                  pl.BlockSpec((POOL_HALO, d),
                               lambda i: (jnp.maximum(i * halo_blocks - 1, 0), 0)),
                  vec, vec, vec,
                  pl.BlockSpec((ng, POOL_GROUP, POOL_GROUP), lambda i: (0, 0, 0)),
                  vec, vec],
        out_specs=pl.BlockSpec((tm, d), lambda i: (i, 0)),
        scratch_shapes=[pltpu.VMEM((tm + POOL_HALO, d), F32)],
        compiler_params=_params("arbitrary"),
        name="pool_layer",
    )(x, x, gain, shift, scale, w_pool, pool_scale, gate)


def _pad_cols(w, n):
    return jnp.pad(w, ((0, 0), (0, n - w.shape[1])))


def _mla_weights(w_dq, w_uq, w_dkv, w_ukv, w_o):
    h = MLA_HEADS
    w_uq_h = w_uq.reshape(MLA_Q_LORA, h, MLA_D_NOPE + MLA_D_ROPE)
    w_uq_pad = jnp.pad(w_uq_h, ((0, 0), (0, 0), (0, MLA_D_QK_PAD - MLA_D_NOPE - MLA_D_ROPE)))
    w_uq_pad = w_uq_pad.reshape(MLA_Q_LORA, h * MLA_D_QK_PAD).astype(BF16)
    w_ukv_h = w_ukv.reshape(MLA_KV_LORA, h, MLA_D_NOPE + MLA_D_V)
    w_uk = w_ukv_h[:, :, :MLA_D_NOPE].reshape(MLA_KV_LORA, h * MLA_D_NOPE).astype(BF16)
    w_uv = w_ukv_h[:, :, MLA_D_NOPE:].reshape(MLA_KV_LORA, h * MLA_D_V).astype(BF16)
    w_dkv_pad = _pad_cols(w_dkv, MLA_KVA_PAD).astype(BF16)
    return w_dq.astype(BF16), w_uq_pad, w_dkv_pad, w_uk, w_uv, w_o.astype(BF16)


def _mla_layer(x, h, rope, w_dq, q_norm, w_uq, w_dkv, kv_norm, w_ukv, w_o, gate):
    w_dq_b, w_uq_pad, w_dkv_pad, w_uk, w_uv, w_o_b = _mla_weights(w_dq, w_uq, w_dkv, w_ukv, w_o)
    cq, ckv, k_rope = _mla_down(h, w_dq_b, w_dkv_pad, q_norm[None, :], kv_norm[None, :], *rope)
    q = _mla_q_up(cq, w_uq_pad, *rope)
    k, v = _mla_kv_up(ckv, w_uk, w_uv, k_rope)
    o = _mla_attention(q, k, v, tb=1024, nsub=2)
    return _mm_resid(o, w_o_b, x, gate, tm=1024, tn=1024, tk=o.shape[1], name="mla_out")


def _sb_layer(x, h, w_qkv, w_o, gate):
    n = w_qkv.shape[1]
    colscale = jnp.where(jnp.arange(n) < SB_HEADS * SB_HEAD_DIM,
                         SB_HEAD_DIM ** -0.5 * LOG2_E, 1.0).astype(F32)[None, :]
    qkv = _mm_scale(h, w_qkv.astype(BF16), colscale, tm=1024, tn=1024, name="sb_qkv")
    o = _sb_attention(qkv, tb=256, nsub=4)
    return _mm_resid(o, w_o.astype(BF16), x, gate, tm=1024, tn=1024, tk=o.shape[1],
                     name="sb_out")


def _cast_pad_cols_kernel(w_ref, o_ref):
    n = w_ref.shape[1]
    o_ref[:, :n] = w_ref[...].astype(o_ref.dtype)
    o_ref[:, n:] = jnp.zeros((o_ref.shape[0], o_ref.shape[1] - n), o_ref.dtype)


def _cast_pad_cols(w, n_pad):
    nl, k, n = w.shape
    tr = 256
    return pl.pallas_call(
        _cast_pad_cols_kernel,
        out_shape=jax.ShapeDtypeStruct((nl, k, n_pad), BF16),
        grid=(nl, k // tr),
        in_specs=[pl.BlockSpec((None, tr, n), lambda l, i: (l, i, 0))],
        out_specs=pl.BlockSpec((None, tr, n_pad), lambda l, i: (l, i, 0)),
        compiler_params=_params("arbitrary", "arbitrary"),
        name="cast_pad_cols",
    )(w)


def _cast_pad_rows_kernel(w_ref, o_ref, *, n_in_blocks):
    i = pl.program_id(1)

    @pl.when(i < n_in_blocks)
    def _():
        o_ref[...] = w_ref[...].astype(o_ref.dtype)

    @pl.when(i >= n_in_blocks)
    def _():
        o_ref[...] = jnp.zeros(o_ref.shape, o_ref.dtype)


def _cast_pad_rows(w, k_pad):
    nl, k, n = w.shape
    tr = 256
    n_in = k // tr
    return pl.pallas_call(
        functools.partial(_cast_pad_rows_kernel, n_in_blocks=n_in),
        out_shape=jax.ShapeDtypeStruct((nl, k_pad, n), BF16),
        grid=(nl, k_pad // tr),
        in_specs=[pl.BlockSpec((None, tr, n), lambda l, i: (l, jnp.minimum(i, n_in - 1), 0))],
        out_specs=pl.BlockSpec((None, tr, n), lambda l, i: (l, i, 0)),
        compiler_params=_params("arbitrary", "arbitrary"),
        name="cast_pad_rows",
    )(w)


def _ffn_weights(w_gate, w_up, w_down):
    return (_cast_pad_cols(w_gate, D_FF_PAD), _cast_pad_cols(w_up, D_FF_PAD),
            _cast_pad_rows(w_down, D_FF_PAD))


def _ffn_layer(x, h, ffn_w, layer, gate):
    wg, wu, wd = ffn_w
    act = _ffn_up(h, wg, wu, layer, tm=1024, tn=512)
    return _mm_resid(act, wd, x, gate, tm=512, tn=512, tk=D_FF_PAD, name="ffn_down",
                     layer=layer)


def kernel(x, c, positions, ada_w, ada_table, norm_mix, norm_ffn, norm_final,
           mla_w_dq, mla_q_norm, mla_w_uq, mla_w_dkv, mla_kv_norm, mla_w_ukv, mla_w_o,
           sb_w_qkv, sb_w_o, pool_w, pool_scale, ffn_w_gate, ffn_w_up, ffn_w_down):
    b, s, d = x.shape
    assert b == 1 and c.shape == (1, d)
    xs = x.reshape(s, d)
    mod = _ada_mod(c, ada_w, ada_table).reshape(DEPTH, N_MOD, 1, d)
    rope = _rope_tables(positions.reshape(s))
    ffn_w = _ffn_weights(ffn_w_gate, ffn_w_up, ffn_w_down)
    for i in range(DEPTH):
        shift_m, scale_m, gate_m = mod[i, 0], mod[i, 1], mod[i, 2]
        shift_f, scale_f, gate_f = mod[i, 3], mod[i, 4], mod[i, 5]
        gain_m = norm_mix[i][None, :]
        kind, j = i % N_MIXERS, i // N_MIXERS
        if kind == 0:
            h = _modulate(xs, gain_m, shift_m, scale_m)
            xs = _mla_layer(xs, h, rope, mla_w_dq[j], mla_q_norm[j], mla_w_uq[j],
                            mla_w_dkv[j], mla_kv_norm[j], mla_w_ukv[j], mla_w_o[j], gate_m)
        elif kind == 1:
            h = _modulate(xs, gain_m, shift_m, scale_m)
            xs = _sb_layer(xs, h, sb_w_qkv[j], sb_w_o[j], gate_m)
        else:
            xs = _pool_layer(xs, gain_m, shift_m, scale_m, pool_w[j].astype(BF16),
                             pool_scale[j][None, :], gate_m)
        h = _modulate(xs, norm_ffn[i][None, :], shift_f, scale_f)
        xs = _ffn_layer(xs, h, ffn_w, i, gate_f)
    return _final_norm(xs, norm_final[None, :]).reshape(b, s, d)
```
